```python
import math
import jax
import jax.numpy as jnp
from jax import lax
import numpy as np

D_MODEL = 1024
BATCH = 8
SEQ = 8192
DEPTH = 2

GRID_W = 64
CTX_LEN = 256
EPS = 1e-6

HY_WIDTH = 384
HY_ORDER = 2
HY_BANDS = 8
HY_FEAT = 1 + 2 * HY_BANDS
HY_FILTER_HIDDEN = 64
HY_FILTER_OUT = 2 * HY_ORDER * HY_WIDTH
HY_MIN_DECAY = -math.log(1e-2) / 1.5
HY_MAX_DECAY = -math.log(1e-2) / 0.3

S5_WIDTH = 384
S5_GROUP = 16
S5_GROUPS = S5_WIDTH // S5_GROUP
S5_STATE = 64
S5_MIN_STEP = 1e-3
S5_MAX_STEP = 1e-1

MLA_HEADS = 8
MLA_NOPE = 64
MLA_ROPE = 32
MLA_V = 64
MLA_Q_LORA = 512
MLA_KV_LORA = 256
MLA_WIDTH = MLA_HEADS * MLA_V
MLA_SCALE = (MLA_NOPE + MLA_ROPE) ** -0.5
ROPE_THETA = 10000.0
Q_BLOCK = 128

FFN_HIDDEN = 2816

N_BRANCH = 3
COL_S5 = 0
COL_KV = COL_S5 + S5_WIDTH
COL_KR = COL_KV + MLA_KV_LORA
COL_Q = COL_KR + MLA_ROPE
COL_HY = COL_Q + MLA_Q_LORA
COL_GATE = COL_HY + (HY_ORDER + 1) * HY_WIDTH
IN_WIDTH = COL_GATE + N_BRANCH * D_MODEL
CTX_SIDE_COLS = COL_Q

kernel_name = "hybrid_hyena_s5_mla_prefix_dit"


def rmsnorm(x, g):
    xf = x.astype(jnp.float32)
    y = xf * lax.rsqrt(jnp.mean(xf * xf, axis=-1, keepdims=True) + EPS)
    return (y * g.astype(jnp.float32)).astype(x.dtype)


def dwconv3(x, w, b):
    xp = jnp.pad(x, ((0, 0), (1, 1), (0, 0)))
    return xp[:, :-2] * w[0] + xp[:, 1:-1] * w[1] + xp[:, 2:] * w[2] + b


def hyena_filters(L, f_w1, f_b1, f_w2, f_b2, f_w3, f_freq, f_decay):
    t = jnp.arange(L, dtype=jnp.float32)[:, None]
    t_norm = t / L
    bands = jnp.arange(1, HY_BANDS + 1, dtype=jnp.float32)
    ang = (2.0 * math.pi / L) * t * bands
    feats = jnp.concatenate([t_norm, jnp.cos(ang), jnp.sin(ang)], axis=-1)
    freq = f_freq.astype(jnp.float32)
    z = jnp.sin(freq * (feats @ f_w1.astype(jnp.float32) + f_b1.astype(jnp.float32)))
    z = jnp.sin(freq * (z @ f_w2.astype(jnp.float32) + f_b2.astype(jnp.float32)))
    h = (z @ f_w3.astype(jnp.float32)) * jnp.exp(-t_norm * jnp.abs(f_decay.astype(jnp.float32)))
    h = h.reshape(L, 2, HY_ORDER, HY_WIDTH)
    h = h / jnp.sum(jnp.abs(h), axis=(0, 1), keepdims=True)
    return h[:, 0], h[:, 1]


def bidir_long_conv(u, hf, hb, bias):
    L = u.shape[1]
    C = u.shape[2]
    taps = jnp.concatenate([hf, jnp.zeros((1, C), hf.dtype), hb[:0:-1]], axis=0)
    u_f = jnp.fft.rfft(u.astype(jnp.float32), n=2 * L, axis=1)
    t_f = jnp.fft.rfft(taps, n=2 * L, axis=0)
    y = jnp.fft.irfft(u_f * t_f[None], n=2 * L, axis=1)[:, :L]
    return (y + u.astype(jnp.float32) * bias.astype(jnp.float32)).astype(u.dtype)


def hyena(p, conv_w, conv_b, hf, hb, bias):
    z = dwconv3(p, conv_w, conv_b)
    parts = jnp.split(z, HY_ORDER + 1, axis=-1)
    u = parts[0]
    for o in range(HY_ORDER):
        u = parts[o + 1] * bidir_long_conv(u, hf[:, o], hb[:, o], bias[o])
    return u


def s5_discretize(lam_re, lam_im, log_step, b_re, b_im):
    lam = lax.complex(lam_re.astype(jnp.float32), lam_im.astype(jnp.float32))
    step = jnp.exp(log_step.astype(jnp.float32))[:, None]
    a_bar = jnp.exp(lam * step)
    b_bar = ((a_bar - 1.0) / lam)[..., None] * lax.complex(b_re.astype(jnp.float32), b_im.astype(jnp.float32))
    return a_bar, b_bar


def _linear_recurrence_combine(left, right):
    a_l, b_l = left
    a_r, b_r = right
    return a_r * a_l, a_r * b_l + b_r


def s5_states(u, a_bar, b_bar, h0):
    L = u.shape[1]
    bu = jnp.einsum('blgc,gnc->blgn', u.astype(jnp.complex64), b_bar)
    if h0 is not None:
        bu = bu.at[:, 0].add(a_bar * h0)
    a = jnp.broadcast_to(a_bar, (1, L) + a_bar.shape)
    _, h = lax.associative_scan(_linear_recurrence_combine, (a, bu), axis=1)
    return h


def s5_readout(h, c_mat):
    return jnp.real(jnp.einsum('blgn,gcn->blgc', h, c_mat))


def s5_glu(y, w_glu):
    y = jax.nn.gelu(y)
    a, g = jnp.split(y @ w_glu.astype(jnp.float32), 2, axis=-1)
    return a * jax.nn.sigmoid(g)


def _flip(s, flip):
    return s[:, ::-1] if flip else s


def s5_branch(u, uc, lam_re, lam_im, log_step, b_re, b_im, c_re, c_im, d_skip, w_glu, with_ctx_out):
    B, L, _ = u.shape
    Lc = uc.shape[1]
    uf = u.astype(jnp.float32)
    ucf = uc.astype(jnp.float32)
    ug = uf.reshape(B, L, S5_GROUPS, S5_GROUP)
    ucg = ucf.reshape(B, Lc, S5_GROUPS, S5_GROUP)
    d32 = d_skip.astype(jnp.float32)
    y = d32 * uf
    yc = d32 * ucf if with_ctx_out else None
    for d in range(2):
        a_bar, b_bar = s5_discretize(lam_re[d], lam_im[d], log_step[d], b_re[d], b_im[d])
        c_mat = lax.complex(c_re[d].astype(jnp.float32), c_im[d].astype(jnp.float32))
        flip = d == 1
        hc = s5_states(_flip(ucg, flip), a_bar, b_bar, None)
        h = s5_states(_flip(ug, flip), a_bar, b_bar, hc[:, -1])
        y = y + _flip(s5_readout(h, c_mat), flip).reshape(B, L, S5_WIDTH)
        if with_ctx_out:
            yc = yc + _flip(s5_readout(hc, c_mat), flip).reshape(B, Lc, S5_WIDTH)
    out = s5_glu(y, w_glu).astype(u.dtype)
    out_c = s5_glu(yc, w_glu).astype(uc.dtype) if with_ctx_out else None
    return out, out_c


def axial_rope_tables(L):
    rows = L // GRID_W
    row = jnp.repeat(jnp.arange(rows, dtype=jnp.float32), GRID_W)
    col = jnp.tile(jnp.arange(GRID_W, dtype=jnp.float32), rows)
    n_ax = MLA_ROPE // 4
    inv = ROPE_THETA ** (-jnp.arange(n_ax, dtype=jnp.float32) / n_ax)
    ang = jnp.concatenate([row[:, None] * inv, col[:, None] * inv], axis=-1)
    return jnp.cos(ang), jnp.sin(ang)


def apply_rope(x, cos, sin):
    half = x.shape[-1] // 2
    x1 = x[..., :half].astype(jnp.float32)
    x2 = x[..., half:].astype(jnp.float32)
    c = cos[None, :, None, :]
    s = sin[None, :, None, :]
    return jnp.concatenate([x1 * c - x2 * s, x1 * s + x2 * c], axis=-1).astype(x.dtype)


def mla_keys_values(p_kv, p_kr, g_kv, w_ukv, rope):
    B, L, _ = p_kv.shape
    kv = (rmsnorm(p_kv, g_kv) @ w_ukv).reshape(B, L, MLA_HEADS, MLA_NOPE + MLA_V)
    k_nope, v = kv[..., :MLA_NOPE], kv[..., MLA_NOPE:]
    k_rope = p_kr[:, :, None, :]
    if rope is not None:
        k_rope = apply_rope(k_rope, rope[0], rope[1])
    k = jnp.concatenate([k_nope, jnp.broadcast_to(k_rope, (B, L, MLA_HEADS, MLA_ROPE))], axis=-1)
    return k, v


def mla_queries(p_q, g_q, w_uq, rope):
    B, L, _ = p_q.shape
    q = (rmsnorm(p_q, g_q) @ w_uq).reshape(B, L, MLA_HEADS, MLA_NOPE + MLA_ROPE)
    q_nope, q_rope = q[..., :MLA_NOPE], q[..., MLA_NOPE:]
    if rope is not None:
        q_rope = apply_rope(q_rope, rope[0], rope[1])
    return jnp.concatenate([q_nope, q_rope], axis=-1) * MLA_SCALE


def attention(q, k, v):
    s = jnp.einsum('bqhd,bkhd->bhqk', q, k, preferred_element_type=jnp.float32)
    p = jax.nn.softmax(s, axis=-1)
    return jnp.einsum('bhqk,bkhd->bqhd', p.astype(v.dtype), v)


def blocked_attention(q, k, v):
    B, L, H, Dk = q.shape
    nb = L // Q_BLOCK
    qb = q.reshape(B, nb, Q_BLOCK, H, Dk).transpose(1, 0, 2, 3, 4)
    ob = lax.map(lambda q_blk: attention(q_blk, k, v), qb)
    return ob.transpose(1, 0, 2, 3, 4).reshape(B, L, H, ob.shape[-1])


def merge_branches(p_gate, y_hy, y_s5, y_mla, w_br_hy, w_br_s5, w_br_mla, w_o):
    g_hy, g_s5, g_mla = jnp.split(jax.nn.sigmoid(p_gate), N_BRANCH, axis=-1)
    merged = g_hy * (y_hy @ w_br_hy) + g_s5 * (y_s5 @ w_br_s5) + g_mla * (y_mla @ w_br_mla)
    return merged @ w_o


def conv_ffn(h, w_up, conv_w, conv_b, w_down):
    a = dwconv3(h @ w_up, conv_w, conv_b)
    u, g = jnp.split(a, 2, axis=-1)
    return (jax.nn.silu(u) * g) @ w_down


def setup_inputs(seed: int = 0) -> dict:
    key = jax.random.key(seed)
    ks = iter(jax.random.split(key, 64))

    def nrm(shape, scale):
        return jax.random.normal(next(ks), shape, jnp.float32) * scale

    def gain(shape):
        return 1.0 + nrm(shape, 0.01)

    D = D_MODEL
    G, N, CG = S5_GROUPS, S5_STATE, S5_GROUP
    HID = HY_FILTER_HIDDEN
    n_idx = jnp.arange(N, dtype=jnp.float32)
    return {
        "x": nrm((BATCH, SEQ, D), 1.0),
        "c": nrm((BATCH, D), 1.0),
        "ctx": nrm((BATCH, CTX_LEN, D), 1.0),
        "c_ctx": nrm((D,), 1.0),
        "w_mod": nrm((DEPTH, D, 6 * D), 0.5 * D ** -0.5),
        "b_mod": nrm((DEPTH, 6 * D), 0.01),
        "norm1_g": gain((DEPTH, D)),
        "norm2_g": gain((DEPTH, D)),
        "w_in": nrm((DEPTH, D, IN_WIDTH), D ** -0.5),
        "hy_conv_w": nrm((DEPTH, 3, (HY_ORDER + 1) * HY_WIDTH), 3 ** -0.5),
        "hy_conv_b": nrm((DEPTH, (HY_ORDER + 1) * HY_WIDTH), 0.01),
        "hy_f_w1": nrm((DEPTH, HY_FEAT, HID), HY_FEAT ** -0.5),
        "hy_f_b1": nrm((DEPTH, HID), 0.1),
        "hy_f_w2": nrm((DEPTH, HID, HID), HID ** -0.5),
        "hy_f_b2": nrm((DEPTH, HID), 0.1),
        "hy_f_w3": nrm((DEPTH, HID, HY_FILTER_OUT), HID ** -0.5),
        "hy_f_freq": 1.0 + nrm((DEPTH, HID), 0.1),
        "hy_f_decay": jax.random.uniform(next(ks), (DEPTH, HY_FILTER_OUT), jnp.float32, HY_MIN_DECAY, HY_MAX_DECAY),
        "hy_bias": nrm((DEPTH, HY_ORDER, HY_WIDTH), 1.0),
        "s5_lam_re": -0.5 + nrm((DEPTH, 2, G, N), 0.01),
        "s5_lam_im": math.pi * n_idx + nrm((DEPTH, 2, G, N), 0.01),
        "s5_log_step": jax.random.uniform(next(ks), (DEPTH, 2, G), jnp.float32, math.log(S5_MIN_STEP), math.log(S5_MAX_STEP)),
        "s5_b_re": nrm((DEPTH, 2, G, N, CG), (2 * CG) ** -0.5),
        "s5_b_im": nrm((DEPTH, 2, G, N, CG), (2 * CG) ** -0.5),
        "s5_c_re": nrm((DEPTH, 2, G, CG, N), N ** -0.5),
        "s5_c_im": nrm((DEPTH, 2, G, CG, N), N ** -0.5),
        "s5_d": nrm((DEPTH, S5_WIDTH), 1.0),
        "s5_w_glu": nrm((DEPTH, S5_WIDTH, 2 * S5_WIDTH), S5_WIDTH ** -0.5),
        "mla_g_q": gain((DEPTH, MLA_Q_LORA)),
        "mla_w_uq": nrm((DEPTH, MLA_Q_LORA, MLA_HEADS * (MLA_NOPE + MLA_ROPE)), MLA_Q_LORA ** -0.5),
        "mla_g_kv": gain((DEPTH, MLA_KV_LORA)),
        "mla_w_ukv": nrm((DEPTH, MLA_KV_LORA, MLA_HEADS * (MLA_NOPE + MLA_V)), MLA_KV_LORA ** -0.5),
        "w_br_hy": nrm((DEPTH, HY_WIDTH, D), HY_WIDTH ** -0.5),
        "w_br_s5": nrm((DEPTH, S5_WIDTH, D), S5_WIDTH ** -0.5),
        "w_br_mla": nrm((DEPTH, MLA_WIDTH, D), MLA_WIDTH ** -0.5),
        "w_o": nrm((DEPTH, D, D), D ** -0.5),
        "ffn_w_up": nrm((DEPTH, D, 2 * FFN_HIDDEN), D ** -0.5),
        "ffn_conv_w": nrm((DEPTH, 3, 2 * FFN_HIDDEN), 3 ** -0.5),
        "ffn_conv_b": nrm((DEPTH, 2 * FFN_HIDDEN), 0.01),
        "ffn_w_down": nrm((DEPTH, FFN_HIDDEN, D), FFN_HIDDEN ** -0.5),
        "final_g": gain((D,)),
    }


def reference(x, c, ctx, c_ctx, w_mod, b_mod, norm1_g, norm2_g, w_in, hy_conv_w, hy_conv_b,
              hy_f_w1, hy_f_b1, hy_f_w2, hy_f_b2, hy_f_w3, hy_f_freq, hy_f_decay, hy_bias,
              s5_lam_re, s5_lam_im, s5_log_step, s5_b_re, s5_b_im, s5_c_re, s5_c_im, s5_d, s5_w_glu,
              mla_g_q, mla_w_uq, mla_g_kv, mla_w_ukv, w_br_hy, w_br_s5, w_br_mla, w_o,
              ffn_w_up, ffn_conv_w, ffn_conv_b, ffn_w_down, final_g):
    B, L, _ = x.shape
    Lc = ctx.shape[1]
    rope = axial_rope_tables(L)
    xc = ctx
    for i in range(DEPTH):
        ctx_out = i < DEPTH - 1
        mod = (jax.nn.silu(c) @ w_mod[i] + b_mod[i])[:, None, :]
        mod_c = jax.nn.silu(c_ctx) @ w_mod[i] + b_mod[i]
        sh1, sc1, g1, sh2, sc2, g2 = jnp.split(mod, 6, axis=-1)
        sh1c, sc1c, g1c, sh2c, sc2c, g2c = jnp.split(mod_c, 6, axis=-1)

        h = rmsnorm(x, norm1_g[i]) * (1.0 + sc1) + sh1
        hc = rmsnorm(xc, norm1_g[i]) * (1.0 + sc1c) + sh1c
        p = h @ w_in[i]
        pc = hc @ (w_in[i] if ctx_out else w_in[i][:, :CTX_SIDE_COLS])

        y_s5, yc_s5 = s5_branch(p[..., COL_S5:COL_KV], pc[..., COL_S5:COL_KV],
                                s5_lam_re[i], s5_lam_im[i], s5_log_step[i], s5_b_re[i], s5_b_im[i],
                                s5_c_re[i], s5_c_im[i], s5_d[i], s5_w_glu[i], ctx_out)

        k_lat, v_lat = mla_keys_values(p[..., COL_KV:COL_KR], p[..., COL_KR:COL_Q], mla_g_kv[i], mla_w_ukv[i], rope)
        k_ctx, v_ctx = mla_keys_values(pc[..., COL_KV:COL_KR], pc[..., COL_KR:COL_Q], mla_g_kv[i], mla_w_ukv[i], None)
        q_lat = mla_queries(p[..., COL_Q:COL_HY], mla_g_q[i], mla_w_uq[i], rope)
        y_mla = blocked_attention(q_lat, jnp.concatenate([k_lat, k_ctx], axis=1),
                                  jnp.concatenate([v_lat, v_ctx], axis=1)).reshape(B, L, MLA_WIDTH)

        filt = (hy_f_w1[i], hy_f_b1[i], hy_f_w2[i], hy_f_b2[i], hy_f_w3[i], hy_f_freq[i], hy_f_decay[i])
        hf, hb = hyena_filters(L, *filt)
        y_hy = hyena(p[..., COL_HY:COL_GATE], hy_conv_w[i], hy_conv_b[i], hf, hb, hy_bias[i])

        x = x + g1 * merge_branches(p[..., COL_GATE:], y_hy, y_s5, y_mla,
                                    w_br_hy[i], w_br_s5[i], w_br_mla[i], w_o[i])

        if ctx_out:
            hfc, hbc = hyena_filters(Lc, *filt)
            yc_hy = hyena(pc[..., COL_HY:COL_GATE], hy_conv_w[i], hy_conv_b[i], hfc, hbc, hy_bias[i])
            q_ctx = mla_queries(pc[..., COL_Q:COL_HY], mla_g_q[i], mla_w_uq[i], None)
            yc_mla = attention(q_ctx, k_ctx, v_ctx).reshape(B, Lc, MLA_WIDTH)
            xc = xc + g1c * merge_branches(pc[..., COL_GATE:], yc_hy, yc_s5, yc_mla,
                                           w_br_hy[i], w_br_s5[i], w_br_mla[i], w_o[i])

        x = x + g2 * conv_ffn(rmsnorm(x, norm2_g[i]) * (1.0 + sc2) + sh2,
                              ffn_w_up[i], ffn_conv_w[i], ffn_conv_b[i], ffn_w_down[i])
        if ctx_out:
            xc = xc + g2c * conv_ffn(rmsnorm(xc, norm2_g[i]) * (1.0 + sc2c) + sh2c,
                                     ffn_w_up[i], ffn_conv_w[i], ffn_conv_b[i], ffn_w_down[i])
    return rmsnorm(x, final_g)
```

```python
import functools
import math

import jax
import jax.numpy as jnp
from jax import lax
from jax.experimental import pallas as pl
from jax.experimental.pallas import tpu as pltpu

F32 = jnp.float32
BF16 = jnp.bfloat16

EPS = 1e-6
GRID_W = 64
HY_WIDTH = 384
HY_ORDER = 2
HY_BANDS = 8
S5_WIDTH = 384
S5_GROUP = 16
S5_GROUPS = S5_WIDTH // S5_GROUP
S5_STATE = 64
S5_CHUNK = 64
MLA_HEADS = 8
MLA_NOPE = 64
MLA_ROPE = 32
MLA_V = 64
MLA_SCALE = (MLA_NOPE + MLA_ROPE) ** -0.5
ROPE_THETA = 10000.0
HEAD_PAD = 128
FFT_NB = 128
VMEM_LIMIT = 56 * 1024 * 1024


def _cparams(sem):
    return pltpu.CompilerParams(dimension_semantics=sem, vmem_limit_bytes=VMEM_LIMIT)


def _split(x):
    hi = x.astype(BF16)
    lo = (x - hi.astype(F32)).astype(BF16)
    return hi, lo


def _dot(a, b):
    return jnp.dot(a, b, preferred_element_type=F32)


def _dot3(ah, al, bh, bl):
    return _dot(ah, bh) + _dot(al, bh) + _dot(ah, bl)


def _sigmoid(x):
    return 1.0 / (1.0 + jnp.exp(-x))


def _silu(x):
    return x * _sigmoid(x)


def _rms(x, g):
    return x * lax.rsqrt(jnp.mean(x * x, axis=-1, keepdims=True) + EPS) * g


def _const_spec(shape):
    nd = len(shape)
    return pl.BlockSpec(shape, lambda *_: (0,) * nd)


def _mod_kernel(c_ref, w_ref, b_ref, o_ref):
    ch, cl = _split(_silu(c_ref[...]))
    wh, wl = _split(w_ref[...])
    o_ref[...] = _dot3(ch, cl, wh, wl) + b_ref[...]


def _modulation(c_all, w, b):
    m, d = c_all.shape
    n = w.shape[1]
    tn = 1024
    return pl.pallas_call(
        _mod_kernel,
        grid=(n // tn,),
        in_specs=[pl.BlockSpec((m, d), lambda j: (0, 0)),
                  pl.BlockSpec((d, tn), lambda j: (0, j)),
                  pl.BlockSpec((1, tn), lambda j: (0, j))],
        out_specs=pl.BlockSpec((m, tn), lambda j: (0, j)),
        out_shape=jax.ShapeDtypeStruct((m, n), F32),
        compiler_params=_cparams(("arbitrary",)),
    )(c_all, w, b.reshape(1, n))


def _rope(x, cos, sa, sb):
    return (x * cos + pltpu.roll(x, HEAD_PAD - MLA_ROPE // 2, 1) * sa
            + pltpu.roll(x, MLA_ROPE // 2, 1) * sb)


def _in_kernel(full, x_ref, ge_ref, sh_ref, cos_ref, sa_ref, sb_ref,
               w_s5, w_kv, w_kr, gkv_ref, w_uk, w_uv, *rest):
    if full:
        (w_q, gq_ref, w_uq, w_hy, w_gate,
         o_s5, o_k, o_v, o_q, o_hy, o_gate) = rest
    else:
        o_s5, o_k, o_v = rest
    hb = (_rms(x_ref[...], ge_ref[...]) + sh_ref[...]).astype(BF16)
    cos, sa, sb = cos_ref[...], sa_ref[...], sb_ref[...]
    o_s5[...] = _dot(hb, w_s5[...]).astype(o_s5.dtype)
    nkv = _rms(_dot(hb, w_kv[...]), gkv_ref[...]).astype(BF16)
    o_v[...] = _dot(nkv, w_uv[...]).astype(o_v.dtype)
    knope = _dot(nkv, w_uk[...])
    kr = _rope(_dot(hb, w_kr[...]), cos, sa, sb)
    for h in range(MLA_HEADS):
        o_k[h] = (knope[:, h * HEAD_PAD:(h + 1) * HEAD_PAD] + kr).astype(o_k.dtype)
    if full:
        nq = _rms(_dot(hb, w_q[...]), gq_ref[...]).astype(BF16)
        qf = _dot(nq, w_uq[...])
        for h in range(MLA_HEADS):
            qh = _rope(qf[:, h * HEAD_PAD:(h + 1) * HEAD_PAD], cos, sa, sb) * MLA_SCALE
            o_q[h] = qh.astype(o_q.dtype)
        o_hy[...] = _dot(hb, w_hy[...]).astype(o_hy.dtype)
        o_gate[...] = _dot(hb, w_gate[...]).astype(o_gate.dtype)


def _in_proj(x, ge, sh, rope_tabs, wts, full, tm):
    B, L, D = x.shape
    H = MLA_HEADS
    tok = lambda n: pl.BlockSpec((None, tm, n), lambda b, i: (b, i, 0))
    per_b = pl.BlockSpec((None, 1, D), lambda b, i: (b, 0, 0))
    tab = pl.BlockSpec((tm, HEAD_PAD), lambda b, i: (i, 0))
    head = pl.BlockSpec((None, H, tm, HEAD_PAD), lambda b, i: (b, 0, i, 0))
    names = ["w_s5", "w_kv", "w_kr", "g_kv", "w_uk", "w_uv"]
    if full:
        names += ["w_q", "g_q", "w_uq", "w_hy", "w_gate"]
    ws = [wts[n] for n in names]
    in_specs = [tok(D), per_b, per_b, tab, tab, tab] + [_const_spec(w.shape) for w in ws]
    out_specs = [tok(S5_WIDTH), head, tok(H * MLA_V)]
    out_shape = [jax.ShapeDtypeStruct((B, L, S5_WIDTH), BF16),
                 jax.ShapeDtypeStruct((B, H, L, HEAD_PAD), BF16),
                 jax.ShapeDtypeStruct((B, L, H * MLA_V), BF16)]
    if full:
        n_hy = (HY_ORDER + 1) * HY_WIDTH
        out_specs += [head, tok(n_hy), tok(3 * D)]
        out_shape += [jax.ShapeDtypeStruct((B, H, L, HEAD_PAD), BF16),
                      jax.ShapeDtypeStruct((B, L, n_hy), BF16),
                      jax.ShapeDtypeStruct((B, L, 3 * D), BF16)]
    return pl.pallas_call(
        functools.partial(_in_kernel, full),
        grid=(B, L // tm),
        in_specs=in_specs, out_specs=out_specs, out_shape=out_shape,
        compiler_params=_cparams(("parallel", "parallel")),
    )(x, ge, sh, *rope_tabs, *ws)


def _attn_kernel(segs, tq, q_ref, *refs):
    o_ref = refs[-1]
    lane = lax.broadcasted_iota(jnp.int32, (tq, HEAD_PAD), 1)
    outs = []
    for hh in range(2):
        q = q_ref[hh]
        carry = (jnp.full((tq, 1), -jnp.inf, F32), jnp.zeros((tq, 1), F32),
                 jnp.zeros((tq, HEAD_PAD), F32))
        for si, (lk, tk) in enumerate(segs):
            k_ref, v_ref = refs[2 * si], refs[2 * si + 1]

            def body(j, c, k_ref=k_ref, v_ref=v_ref, tk=tk):
                m, l, acc = c
                off = pl.multiple_of(j * tk, tk)
                kb = k_ref[hh, pl.ds(off, tk), :]
                s = lax.dot_general(q, kb, (((1,), (1,)), ((), ())), preferred_element_type=F32)
                m_new = jnp.maximum(m, jnp.max(s, axis=-1, keepdims=True))
                alpha = jnp.exp(m - m_new)
                p = jnp.exp(s - m_new)
                l = alpha * l + jnp.sum(p, axis=-1, keepdims=True)
                acc = alpha * acc + _dot(p.astype(BF16), v_ref[pl.ds(off, tk), :])
                return m_new, l, acc

            carry = lax.fori_loop(0, lk // tk, body, carry)
        outs.append(carry[2] / carry[1])
    o_ref[...] = jnp.where(lane < MLA_V, outs[0], outs[1]).astype(o_ref.dtype)


def _attention(q, kvs, tq, tks):
    B, H, Lq, _ = q.shape
    segs = tuple((k.shape[2], tk) for (k, _), tk in zip(kvs, tks))
    in_specs = [pl.BlockSpec((None, 2, tq, HEAD_PAD), lambda b, j, i: (b, j, i, 0))]
    args = [q]
    for k, v in kvs:
        lk = k.shape[2]
        in_specs.append(pl.BlockSpec((None, 2, lk, HEAD_PAD), lambda b, j, i: (b, j, 0, 0)))
        in_specs.append(pl.BlockSpec((None, lk, 2 * MLA_V), lambda b, j, i: (b, 0, j)))
        args += [k, v]
    return pl.pallas_call(
        functools.partial(_attn_kernel, segs, tq),
        grid=(B, H // 2, Lq // tq),
        in_specs=in_specs,
        out_specs=pl.BlockSpec((None, tq, 2 * MLA_V), lambda b, j, i: (b, i, j)),
        out_shape=jax.ShapeDtypeStruct((B, Lq, H * MLA_V), BF16),
        compiler_params=_cparams(("parallel", "parallel", "arbitrary")),
    )(*args)


def _s5_kernel(nch, nb, u_ref, mi_ref, min_ref, mout_ref, atr_ref, ati_ref, h0r_ref, h0i_ref,
               y_ref, hfr_ref, hfi_ref, sr_ref, si_ref, hr_ref, hi_ref):
    u = u_ref[...]
    ns = S5_STATE
    s = _dot(u, min_ref[...])
    sr_ref[...] = s[:, :2 * ns]
    si_ref[...] = s[:, 2 * ns:]
    atr, ati = atr_ref[...], ati_ref[...]
    fwd = lax.broadcasted_iota(jnp.int32, (nb, 2 * ns), 1) < ns

    def body(i, c):
        hr, hi = c
        rf = pl.multiple_of(i * nb, nb)
        rb = pl.multiple_of((nch - 1 - i) * nb, nb)
        hr_ref[pl.ds(rf, nb), 0:ns] = hr[:, 0:ns]
        hi_ref[pl.ds(rf, nb), 0:ns] = hi[:, 0:ns]
        hr_ref[pl.ds(rb, nb), ns:2 * ns] = hr[:, ns:2 * ns]
        hi_ref[pl.ds(rb, nb), ns:2 * ns] = hi[:, ns:2 * ns]
        s_r = jnp.where(fwd, sr_ref[pl.ds(rf, nb), :], sr_ref[pl.ds(rb, nb), :])
        s_i = jnp.where(fwd, si_ref[pl.ds(rf, nb), :], si_ref[pl.ds(rb, nb), :])
        return atr * hr - ati * hi + s_r, atr * hi + ati * hr + s_i

    hr, hi = lax.fori_loop(0, nch, body, (h0r_ref[...], h0i_ref[...]))
    hfr_ref[...] = hr
    hfi_ref[...] = hi
    hcat = jnp.concatenate([hr_ref[...], hi_ref[...]], axis=1).astype(BF16)
    y_ref[...] = _dot(u, mi_ref[...]) + _dot(hcat, mout_ref[...])


def _cmul(ar, ai, br, bi):
    return ar * br - ai * bi, ar * bi + ai * br


def _s5_matrices(lam_re, lam_im, log_step, b_re, b_im, c_re, c_im, T):
    hp = lax.Precision.HIGHEST
    step = jnp.exp(log_step)[..., None]
    def apow(tau):
        mag = jnp.exp(lam_re[:, :, None, :] * step[:, :, None, :] * tau[None, None, :, None])
        ang = lam_im[:, :, None, :] * step[:, :, None, :] * tau[None, None, :, None]
        return mag * jnp.cos(ang), mag * jnp.sin(ang)
    a_r, a_i = apow(jnp.ones((1,), F32))
    a_r, a_i = a_r[:, :, 0], a_i[:, :, 0]
    den = lam_re ** 2 + lam_im ** 2
    q_r, q_i = _cmul(a_r - 1.0, a_i, lam_re / den, -lam_im / den)
    bb_r, bb_i = _cmul(q_r[..., None], q_i[..., None], b_re, b_im)
    tt = jnp.arange(T, dtype=F32)
    p_r, p_i = apow(tt)
    cb_r, cb_i = _cmul(jnp.swapaxes(c_re, -1, -2)[..., :, :, None], jnp.swapaxes(c_im, -1, -2)[..., :, :, None],
                       bb_r[..., :, None, :], bb_i[..., :, None, :])
    kk = (jnp.einsum('dgtn,dgncx->dgtcx', p_r, cb_r, precision=hp)
          - jnp.einsum('dgtn,dgncx->dgtcx', p_i, cb_i, precision=hp))
    kf, kb = kk[0], kk[1]
    kall = jnp.concatenate([kb[:, :0:-1], (kf[:, :1] + kb[:, :1]), kf[:, 1:]], axis=1)
    idx = jnp.arange(T)[None, :] - jnp.arange(T)[:, None] + (T - 1)
    m4 = kall[:, idx]
    G = m4.shape[0]
    cg = m4.shape[-1]
    m_intra = jnp.transpose(m4, (0, 1, 4, 2, 3)).reshape(G, T * cg, T * cg)
    def instate(pr, pi, br, bi):
        return _cmul(pr[:, :, None, :], pi[:, :, None, :],
                     jnp.swapaxes(br, -1, -2)[:, None], jnp.swapaxes(bi, -1, -2)[:, None])
    f_r, f_i = instate(p_r[0][:, ::-1], p_i[0][:, ::-1], bb_r[0], bb_i[0])
    g_r, g_i = instate(p_r[1], p_i[1], bb_r[1], bb_i[1])
    m_in = jnp.concatenate([f_r, g_r, f_i, g_i], axis=-1).reshape(G, T * cg, 4 * S5_STATE)
    p1_r, p1_i = apow(tt + 1.0)
    def outstate(pr, pi, cr, ci):
        xr, xi = _cmul(jnp.swapaxes(pr, 1, 2)[:, :, :, None], jnp.swapaxes(pi, 1, 2)[:, :, :, None],
                       jnp.swapaxes(cr, 1, 2)[:, :, None, :], jnp.swapaxes(ci, 1, 2)[:, :, None, :])
        return xr, -xi
    of_r, of_i = outstate(p1_r[0], p1_i[0], c_re[0], c_im[0])
    ob_r, ob_i = outstate(p1_r[1][:, ::-1], p1_i[1][:, ::-1], c_re[1], c_im[1])
    m_out = jnp.concatenate([of_r, ob_r, of_i, ob_i], axis=1).reshape(G, 4 * S5_STATE, T * cg)
    at_r, at_i = apow(jnp.full((1,), float(T), F32))
    at_r = jnp.concatenate([at_r[0, :, 0], at_r[1, :, 0]], axis=-1)[:, None, :]
    at_i = jnp.concatenate([at_i[0, :, 0], at_i[1, :, 0]], axis=-1)[:, None, :]
    return m_intra.astype(BF16), m_in.astype(BF16), m_out.astype(BF16), at_r, at_i


def _s5_scan(u, mats, h0):
    B, L, _ = u.shape
    T, G, cg, ns = S5_CHUNK, S5_GROUPS, S5_GROUP, S5_STATE
    nch = L // T
    R = nch * B
    m_intra, m_in, m_out, at_r, at_i = mats
    ut = jnp.transpose(u.reshape(B, nch, T, G, cg), (3, 1, 0, 2, 4)).reshape(G, R, T * cg)
    gspec = lambda r, c: pl.BlockSpec((None, r, c), lambda g: (g, 0, 0))
    y, hfr, hfi = pl.pallas_call(
        functools.partial(_s5_kernel, nch, B),
        grid=(G,),
        in_specs=[gspec(R, T * cg), gspec(T * cg, T * cg), gspec(T * cg, 4 * ns), gspec(4 * ns, T * cg),
                  gspec(1, 2 * ns), gspec(1, 2 * ns), gspec(B, 2 * ns), gspec(B, 2 * ns)],
        out_specs=[gspec(R, T * cg), gspec(B, 2 * ns), gspec(B, 2 * ns)],
        out_shape=[jax.ShapeDtypeStruct((G, R, T * cg), F32),
                   jax.ShapeDtypeStruct((G, B, 2 * ns), F32),
                   jax.ShapeDtypeStruct((G, B, 2 * ns), F32)],
        scratch_shapes=[pltpu.VMEM((R, 2 * ns), F32)] * 4,
        compiler_params=_cparams(("parallel",)),
    )(ut, m_intra, m_in, m_out, at_r, at_i, h0[0], h0[1])
    y = jnp.transpose(y.reshape(G, nch, B, T, cg), (2, 1, 3, 0, 4)).reshape(B, L, G * cg)
    return y, (hfr, hfi)


def _shift_rows(x, prev_row, next_row):
    n = x.shape[0]
    row = lax.broadcasted_iota(jnp.int32, x.shape, 0)
    xm = jnp.where(row == 0, prev_row, pltpu.roll(x, 1, 0))
    xp = jnp.where(row == n - 1, next_row, pltpu.roll(x, n - 1, 0))
    return xm, xp


def _dwconv_kernel(p_ref, w_ref, b_ref, o_ref):
    x = p_ref[...].astype(F32)
    w = w_ref[...]
    xm, xp = _shift_rows(x, 0.0, 0.0)
    o_ref[...] = xm * w[0:1] + x * w[1:2] + xp * w[2:3] + b_ref[...]


def _hy_dwconv(p, w, b):
    B, L, n = p.shape
    tc = 128
    nblk = HY_WIDTH // tc
    outs = []
    for part in range(n // HY_WIDTH):
        outs.append(pl.pallas_call(
            _dwconv_kernel,
            grid=(B, nblk),
            in_specs=[pl.BlockSpec((None, L, tc), lambda bb, j, part=part: (bb, 0, part * nblk + j)),
                      pl.BlockSpec((3, tc), lambda bb, j, part=part: (0, part * nblk + j)),
                      pl.BlockSpec((1, tc), lambda bb, j, part=part: (0, part * nblk + j))],
            out_specs=pl.BlockSpec((None, L, tc), lambda bb, j: (bb, 0, j)),
            out_shape=jax.ShapeDtypeStruct((B, L, HY_WIDTH), F32),
            compiler_params=_cparams(("parallel", "parallel")),
        )(p, w, b.reshape(1, n)))
    return outs


def _fft1_kernel(fh_ref, fl_ref, u_ref, o_ref):
    uh, ul = _split(u_ref[...])
    o_ref[...] = _dot3(fh_ref[...], fl_ref[...], uh, ul)


def _fft1(u2, f1):
    B, K, W = u2.shape
    M = f1[0].shape[0]
    tn = min(W, 6144)
    return pl.pallas_call(
        _fft1_kernel,
        grid=(B, W // tn),
        in_specs=[_const_spec((M, K)), _const_spec((M, K)),
                  pl.BlockSpec((None, K, tn), lambda b, j: (b, 0, j))],
        out_specs=pl.BlockSpec((None, M, tn), lambda b, j: (b, 0, j)),
        out_shape=jax.ShapeDtypeStruct((B, M, W), F32),
        compiler_params=_cparams(("parallel", "parallel")),
    )(f1[0], f1[1], u2)


def _fft2_kernel(conv, nbat, fh_ref, fl_ref, *rest):
    nb = FFT_NB
    if conv:
        gh_ref, gl_ref, h_ref, a_ref, o_ref = rest
        hsp = h_ref[...]
        hr, hi = hsp[:nb], hsp[nb:]
    else:
        a_ref, o_ref = rest
    for b in range(nbat):
        a = jnp.concatenate([a_ref[b, 0], a_ref[b, 1]], axis=0)
        ah, al = _split(a)
        x = _dot3(fh_ref[...], fl_ref[...], ah, al)
        if not conv:
            o_ref[b] = x
            continue
        xr, xi = x[:nb], x[nb:]
        y = jnp.concatenate([xr * hr - xi * hi, xr * hi + xi * hr], axis=0)
        yh, yl = _split(y)
        z = _dot3(gh_ref[...], gl_ref[...], yh, yl)
        o_ref[b, 0] = z[:nb]
        o_ref[b, 1] = z[nb:]


def _fft2_spectrum(a, fk):
    Bf, _, Na, Nb, C = a.shape
    mat = pl.BlockSpec((None, 2 * Nb, 2 * Nb), lambda k: (k, 0, 0))
    return pl.pallas_call(
        functools.partial(_fft2_kernel, False, Bf),
        grid=(Na,),
        in_specs=[mat, mat, pl.BlockSpec((Bf, 2, None, Nb, C), lambda k: (0, 0, k, 0, 0))],
        out_specs=pl.BlockSpec((Bf, None, 2 * Nb, C), lambda k: (0, k, 0, 0)),
        out_shape=jax.ShapeDtypeStruct((Bf, Na, 2 * Nb, C), F32),
        compiler_params=_cparams(("parallel",)),
    )(fk[0], fk[1], a)


def _fft2_conv(a, fk, gk, hspec):
    B, _, Na, Nb, C = a.shape
    mat = pl.BlockSpec((None, 2 * Nb, 2 * Nb), lambda k: (k, 0, 0))
    blk = pl.BlockSpec((B, 2, None, Nb, C), lambda k: (0, 0, k, 0, 0))
    return pl.pallas_call(
        functools.partial(_fft2_kernel, True, B),
        grid=(Na,),
        in_specs=[mat, mat, mat, mat, pl.BlockSpec((None, 2 * Nb, C), lambda k: (k, 0, 0)), blk],
        out_specs=blk,
        out_shape=jax.ShapeDtypeStruct(a.shape, F32),
        compiler_params=_cparams(("parallel",)),
    )(fk[0], fk[1], gk[0], gk[1], hspec, a)


def _fft3_kernel(fh_ref, fl_ref, b_ref, u_ref, bias_ref, x_ref, o_ref):
    bh, bl = _split(b_ref[...])
    y = _dot3(fh_ref[...], fl_ref[...], bh, bl)
    o_ref[...] = (x_ref[...] * (y + u_ref[...] * bias_ref[...])).astype(o_ref.dtype)


def _fft3(bmat, f3, u2, bias_t, x2):
    B, M2, W = bmat.shape
    K = u2.shape[1]
    tn = min(W, 6144)
    tok = pl.BlockSpec((None, K, tn), lambda b, j: (b, 0, j))
    return pl.pallas_call(
        _fft3_kernel,
        grid=(B, W // tn),
        in_specs=[_const_spec((K, M2)), _const_spec((K, M2)),
                  pl.BlockSpec((None, M2, tn), lambda b, j: (b, 0, j)), tok,
                  pl.BlockSpec((1, tn), lambda b, j: (0, j)), tok],
        out_specs=tok,
        out_shape=jax.ShapeDtypeStruct(u2.shape, F32),
        compiler_params=_cparams(("parallel", "parallel")),
    )(f3[0], f3[1], bmat, u2, bias_t, x2)


def _angles(rows, cols, n):
    prod = (rows[:, None] * cols[None, :]) % n
    return prod.astype(F32) * (2.0 * math.pi / n)


def _fft_consts(L):
    N = 2 * L
    Nb = FFT_NB
    Na = N // Nb
    ia = jnp.arange(Na, dtype=jnp.int32)
    ib = jnp.arange(Nb, dtype=jnp.int32)
    th1 = _angles(ia, ia, Na)
    f1 = jnp.concatenate([jnp.cos(th1), -jnp.sin(th1)], axis=0)
    kfull = ia[:, None] + Na * ib[None, :]
    th2 = ((kfull[:, :, None] * ib[None, None, :]) % N).astype(F32) * (2.0 * math.pi / N)
    c2, s2 = jnp.cos(th2), jnp.sin(th2)
    fk = jnp.concatenate([jnp.concatenate([c2, s2], axis=2),
                          jnp.concatenate([-s2, c2], axis=2)], axis=1)
    c2t, s2t = jnp.swapaxes(c2, 1, 2), jnp.swapaxes(s2, 1, 2)
    gk = jnp.concatenate([jnp.concatenate([c2t, -s2t], axis=2),
                          jnp.concatenate([s2t, c2t], axis=2)], axis=1)
    th3 = _angles(ia[:Na // 2], ia, Na)
    f3 = jnp.concatenate([jnp.cos(th3), -jnp.sin(th3)], axis=1) / N
    return dict(f1_half=_split(f1[:, :Na // 2]), f1_full=_split(f1), fk=_split(fk), gk=_split(gk),
                f3=_split(f3), Na=Na, Nb=Nb)


def _dense_kernel(conv, fh_ref, fl_ref, u_ref, *rest):
    u = u_ref[...]
    uh, ul = _split(u)
    x = _dot3(fh_ref[...], fl_ref[...], uh, ul)
    if not conv:
        rest[0][...] = x
        return
    gh_ref, gl_ref, h_ref, bias_ref, x_ref, o_ref = rest
    n = x.shape[0] // 2
    hsp = h_ref[...]
    xr, xi, hr, hi = x[:n], x[n:], hsp[:n], hsp[n:]
    y = jnp.concatenate([xr * hr - xi * hi, xr * hi + xi * hr], axis=0)
    yh, yl = _split(y)
    z = _dot3(gh_ref[...], gl_ref[...], yh, yl)
    o_ref[...] = (x_ref[...] * (z + u * bias_ref[...])).astype(o_ref.dtype)


def _dense_consts(L):
    N = 2 * L
    i_n = jnp.arange(N, dtype=jnp.int32)
    th = _angles(i_n, i_n, N)
    fd = jnp.concatenate([jnp.cos(th), -jnp.sin(th)], axis=0)
    tht = _angles(i_n[:L], i_n, N)
    fi = jnp.concatenate([jnp.cos(tht), -jnp.sin(tht)], axis=1) / N
    return dict(fd_half=_split(fd[:, :L]), fd_full=_split(fd), fi=_split(fi))


def _dense_spectrum(taps, fd):
    Bf, N, C = taps.shape
    return pl.pallas_call(
        functools.partial(_dense_kernel, False),
        grid=(Bf,),
        in_specs=[_const_spec(fd[0].shape), _const_spec(fd[0].shape),
                  pl.BlockSpec((None, N, C), lambda b: (b, 0, 0))],
        out_specs=pl.BlockSpec((None, 2 * N, C), lambda b: (b, 0, 0)),
        out_shape=jax.ShapeDtypeStruct((Bf, 2 * N, C), F32),
        compiler_params=_cparams(("parallel",)),
    )(fd[0], fd[1], taps)


def _dense_conv(u, fd, fi, hspec, bias, xg):
    B, L, C = u.shape
    tok = pl.BlockSpec((None, L, C), lambda b: (b, 0, 0))
    return pl.pallas_call(
        functools.partial(_dense_kernel, True),
        grid=(B,),
        in_specs=[_const_spec(fd[0].shape), _const_spec(fd[0].shape), tok,
                  _const_spec(fi[0].shape), _const_spec(fi[0].shape),
                  _const_spec(hspec.shape), _const_spec((1, C)), tok],
        out_specs=tok,
        out_shape=jax.ShapeDtypeStruct((B, L, C), F32),
        compiler_params=_cparams(("parallel",)),
    )(fd[0], fd[1], u, fi[0], fi[1], hspec, bias.reshape(1, C), xg)


def _hyena_filters(L, f_w1, f_b1, f_w2, f_b2, f_w3, f_freq, f_decay):
    hp = lax.Precision.HIGHEST
    t = jnp.arange(L, dtype=F32)[:, None]
    t_norm = t / L
    bands = jnp.arange(1, HY_BANDS + 1, dtype=F32)
    ang = (2.0 * math.pi / L) * t * bands
    feats = jnp.concatenate([t_norm, jnp.cos(ang), jnp.sin(ang)], axis=-1)
    z = jnp.sin(f_freq * (jnp.dot(feats, f_w1, precision=hp) + f_b1))
    z = jnp.sin(f_freq * (jnp.dot(z, f_w2, precision=hp) + f_b2))
    h = jnp.dot(z, f_w3, precision=hp) * jnp.exp(-t_norm * jnp.abs(f_decay))
    h = h.reshape(L, 2, HY_ORDER, HY_WIDTH)
    h = h / jnp.sum(jnp.abs(h), axis=(0, 1), keepdims=True)
    hf, hb = h[:, 0], h[:, 1]
    taps = jnp.concatenate([hf, jnp.zeros((1, HY_ORDER, HY_WIDTH), F32), hb[:0:-1]], axis=0)
    return jnp.transpose(taps, (1, 0, 2))


def _hyena(p_hy, conv_w, conv_b, filt, bias):
    B, L, _ = p_hy.shape
    C = HY_WIDTH
    taps = _hyena_filters(L, *filt)
    v, x1, x2 = _hy_dwconv(p_hy, conv_w, conv_b)
    gates = (x1, x2)
    u = v
    if 2 * L <= 1024:
        cst = _dense_consts(L)
        spec = _dense_spectrum(taps, cst["fd_full"])
        for o in range(HY_ORDER):
            u = _dense_conv(u, cst["fd_half"], cst["fi"], spec[o], bias[o], gates[o])
        return u
    cst = _fft_consts(L)
    Na, Nb = cst["Na"], cst["Nb"]
    W = Nb * C
    ta = _fft1(taps.reshape(HY_ORDER, Na, W), cst["f1_full"])
    spec = _fft2_spectrum(ta.reshape(HY_ORDER, 2, Na, Nb, C), cst["fk"])
    for o in range(HY_ORDER):
        u2 = u.reshape(B, Na // 2, W)
        a = _fft1(u2, cst["f1_half"])
        bm = _fft2_conv(a.reshape(B, 2, Na, Nb, C), cst["fk"], cst["gk"], spec[o])
        bias_t = jnp.tile(bias[o], Nb).reshape(1, W)
        u = _fft3(bm.reshape(B, 2 * Na, W), cst["f3"], u2, bias_t,
                  gates[o].reshape(B, Na // 2, W)).reshape(B, L, C)
    return u


def _gelu_tanh(x):
    return 0.5 * x * (1.0 + jnp.tanh(math.sqrt(2.0 / math.pi) * (x + 0.044715 * (x * x * x))))


def _merge_kernel(x_ref, g1_ref, pg_ref, yhy_ref, ys5_ref, ps5_ref, d_ref, ymla_ref,
                  wglu_ref, whyr, ws5r, wmlar, wo_ref, o_ref):
    D = x_ref.shape[-1]
    ys = _gelu_tanh(d_ref[...] * ps5_ref[...].astype(F32) + ys5_ref[...])
    ag = _dot(ys.astype(BF16), wglu_ref[...])
    y_s5 = ag[:, :S5_WIDTH] * _sigmoid(ag[:, S5_WIDTH:])
    pg = pg_ref[...].astype(F32)
    merged = (_sigmoid(pg[:, :D]) * _dot(yhy_ref[...].astype(BF16), whyr[...])
              + _sigmoid(pg[:, D:2 * D]) * _dot(y_s5.astype(BF16), ws5r[...])
              + _sigmoid(pg[:, 2 * D:]) * _dot(ymla_ref[...], wmlar[...]))
    o_ref[...] = x_ref[...] + g1_ref[...] * _dot(merged.astype(BF16), wo_ref[...])


def _merge(x, g1, pg, y_hy, ys5, ps5, d_skip, y_mla, wts, tm):
    B, L, D = x.shape
    tok = lambda n: pl.BlockSpec((None, tm, n), lambda b, i: (b, i, 0))
    per_b = pl.BlockSpec((None, 1, D), lambda b, i: (b, 0, 0))
    ws = [wts[n] for n in ("w_glu", "w_br_hy", "w_br_s5", "w_br_mla", "w_o")]
    return pl.pallas_call(
        _merge_kernel,
        grid=(B, L // tm),
        in_specs=[tok(D), per_b, tok(3 * D), tok(HY_WIDTH), tok(S5_WIDTH), tok(S5_WIDTH),
                  _const_spec((1, S5_WIDTH)), tok(MLA_HEADS * MLA_V)] + [_const_spec(w.shape) for w in ws],
        out_specs=tok(D),
        out_shape=jax.ShapeDtypeStruct((B, L, D), F32),
        compiler_params=_cparams(("parallel", "parallel")),
    )(x, g1, pg, y_hy, ys5, ps5, d_skip.reshape(1, -1), y_mla, *ws)


def _ffn_up_kernel(x_ref, ge_ref, sh_ref, w_ref, o_ref):
    hb = (_rms(x_ref[...], ge_ref[...]) + sh_ref[...]).astype(BF16)
    o_ref[...] = _dot(hb, w_ref[...]).astype(o_ref.dtype)


def _ffn_up(x, ge, sh, w_up, tm):
    B, L, D = x.shape
    n = w_up.shape[1]
    per_b = pl.BlockSpec((None, 1, D), lambda b, i: (b, 0, 0))
    return pl.pallas_call(
        _ffn_up_kernel,
        grid=(B, L // tm),
        in_specs=[pl.BlockSpec((None, tm, D), lambda b, i: (b, i, 0)), per_b, per_b, _const_spec(w_up.shape)],
        out_specs=pl.BlockSpec((None, tm, n), lambda b, i: (b, i, 0)),
        out_shape=jax.ShapeDtypeStruct((B, L, n), BF16),
        compiler_params=_cparams(("parallel", "parallel")),
    )(x, ge, sh, w_up)


def _ffn_down_kernel(final, nt, a_ref, ap_ref, an_ref, cw_ref, cb_ref, wd_ref, x_ref, g2_ref, *rest):
    i = pl.program_id(1)
    a = a_ref[...].astype(F32)
    hid = a.shape[1] // 2
    prev_row = jnp.where(i > 0, ap_ref[7:8, :].astype(F32), 0.0)
    next_row = jnp.where(i < nt - 1, an_ref[0:1, :].astype(F32), 0.0)
    am, ap = _shift_rows(a, prev_row, next_row)
    cw = cw_ref[...]
    c = am * cw[0:1] + a * cw[1:2] + ap * cw[2:3] + cb_ref[...]
    act = (_silu(c[:, :hid]) * c[:, hid:]).astype(BF16)
    y = x_ref[...] + g2_ref[...] * _dot(act, wd_ref[...])
    if final:
        fg_ref, o_ref = rest
        o_ref[...] = _rms(y, fg_ref[...])
    else:
        rest[0][...] = y


def _ffn_down(a, conv_w, conv_b, w_down, x, g2, final_g, tm):
    B, L, n = a.shape
    D = x.shape[-1]
    nt = L // tm
    r8 = tm // 8
    final = final_g is not None
    in_specs = [pl.BlockSpec((None, tm, n), lambda b, i: (b, i, 0)),
                pl.BlockSpec((None, 8, n), lambda b, i: (b, jnp.maximum(i * r8 - 1, 0), 0)),
                pl.BlockSpec((None, 8, n), lambda b, i: (b, jnp.minimum((i + 1) * r8, L // 8 - 1), 0)),
                _const_spec((3, n)), _const_spec((1, n)), _const_spec(w_down.shape),
                pl.BlockSpec((None, tm, D), lambda b, i: (b, i, 0)),
                pl.BlockSpec((None, 1, D), lambda b, i: (b, 0, 0))]
    args = [a, a, a, conv_w, conv_b.reshape(1, n), w_down, x, g2]
    if final:
        in_specs.append(_const_spec((1, D)))
        args.append(final_g.reshape(1, D))
    return pl.pallas_call(
        functools.partial(_ffn_down_kernel, final, nt),
        grid=(B, nt),
        in_specs=in_specs,
        out_specs=pl.BlockSpec((None, tm, D), lambda b, i: (b, i, 0)),
        out_shape=jax.ShapeDtypeStruct((B, L, D), F32),
        compiler_params=_cparams(("parallel", "parallel")),
    )(*args)


def _rope_tables(L, rope):
    half = MLA_ROPE // 2
    cos = jnp.ones((L, HEAD_PAD), F32)
    sa = jnp.zeros((L, HEAD_PAD), F32)
    sb = jnp.zeros((L, HEAD_PAD), F32)
    if not rope:
        return cos, sa, sb
    rows = L // GRID_W
    row = jnp.repeat(jnp.arange(rows, dtype=F32), GRID_W)
    col = jnp.tile(jnp.arange(GRID_W, dtype=F32), rows)
    n_ax = MLA_ROPE // 4
    inv = ROPE_THETA ** (-jnp.arange(n_ax, dtype=F32) / n_ax)
    ang = jnp.concatenate([row[:, None] * inv, col[:, None] * inv], axis=-1)
    c, s = jnp.cos(ang), jnp.sin(ang)
    cos = cos.at[:, MLA_NOPE:MLA_NOPE + half].set(c).at[:, MLA_NOPE + half:MLA_NOPE + 2 * half].set(c)
    sa = sa.at[:, MLA_NOPE:MLA_NOPE + half].set(-s)
    sb = sb.at[:, MLA_NOPE + half:MLA_NOPE + 2 * half].set(s)
    return cos, sa, sb


def _layer_weights(w_in, g_q, w_uq, g_kv, w_ukv, s5_w_glu, w_br_hy, w_br_s5, w_br_mla, w_o):
    D = w_in.shape[0]
    H = MLA_HEADS
    c_kv = S5_WIDTH
    c_kr = c_kv + w_ukv.shape[0]
    c_q = c_kr + MLA_ROPE
    c_hy = c_q + w_uq.shape[0]
    c_gate = c_hy + (HY_ORDER + 1) * HY_WIDTH
    bf = lambda a: a.astype(BF16)
    w_kr = jnp.zeros((D, HEAD_PAD), F32).at[:, MLA_NOPE:MLA_NOPE + MLA_ROPE].set(w_in[:, c_kr:c_q])
    ukv = w_ukv.reshape(-1, H, MLA_NOPE + MLA_V)
    w_uk = jnp.zeros((ukv.shape[0], H, HEAD_PAD), F32).at[:, :, :MLA_NOPE].set(ukv[:, :, :MLA_NOPE])
    w_uv = ukv[:, :, MLA_NOPE:]
    uq = w_uq.reshape(-1, H, MLA_NOPE + MLA_ROPE)
    w_uqp = jnp.zeros((uq.shape[0], H, HEAD_PAD), F32).at[:, :, :MLA_NOPE + MLA_ROPE].set(uq)
    return dict(
        w_s5=bf(w_in[:, :c_kv]), w_kv=bf(w_in[:, c_kv:c_kr]), w_kr=bf(w_kr), w_q=bf(w_in[:, c_q:c_hy]),
        w_hy=bf(w_in[:, c_hy:c_gate]), w_gate=bf(w_in[:, c_gate:]),
        g_kv=g_kv.reshape(1, -1), g_q=g_q.reshape(1, -1),
        w_uk=bf(w_uk.reshape(-1, H * HEAD_PAD)), w_uv=bf(w_uv.reshape(-1, H * MLA_V)),
        w_uq=bf(w_uqp.reshape(-1, H * HEAD_PAD)),
        w_glu=bf(s5_w_glu), w_br_hy=bf(w_br_hy), w_br_s5=bf(w_br_s5), w_br_mla=bf(w_br_mla), w_o=bf(w_o))


def _mixer(x, mods, n1g, wts, s5_mats, s5_h0, s5_d, hy_args, rope_tabs, ctx_kv, full, tm, tq, tk):
    sh1, sc1, g1 = mods
    ge = n1g * (1.0 + sc1)
    outs = _in_proj(x, ge, sh1, rope_tabs, wts, full, tm)
    p_s5, k, v = outs[:3]
    y_s5, hfin = _s5_scan(p_s5, s5_mats, s5_h0)
    if not full:
        return None, hfin, (k, v)
    q, p_hy, p_gate = outs[3:]
    kvs = [(k, v)] + ([ctx_kv] if ctx_kv is not None else [])
    tks = [tk] + ([ctx_kv[0].shape[2]] if ctx_kv is not None else [])
    y_mla = _attention(q, kvs, tq, tks)
    y_hy = _hyena(p_hy, *hy_args)
    x_new = _merge(x, g1, p_gate, y_hy, y_s5, p_s5, s5_d, y_mla, wts, tm)
    return x_new, hfin, (k, v)


def _ffn(x, mods, n2g, w_up, conv_w, conv_b, w_down, final_g, tm):
    sh2, sc2, g2 = mods
    a = _ffn_up(x, n2g * (1.0 + sc2), sh2, w_up, tm)
    return _ffn_down(a, conv_w, conv_b, w_down, x, g2, final_g, tm)


def kernel(x, c, ctx, c_ctx, w_mod, b_mod, norm1_g, norm2_g, w_in, hy_conv_w, hy_conv_b, hy_f_w1, hy_f_b1, hy_f_w2, hy_f_b2, hy_f_w3, hy_f_freq, hy_f_decay, hy_bias, s5_lam_re, s5_lam_im, s5_log_step, s5_b_re, s5_b_im, s5_c_re, s5_c_im, s5_d, s5_w_glu, mla_g_q, mla_w_uq, mla_g_kv, mla_w_ukv, w_br_hy, w_br_s5, w_br_mla, w_o, ffn_w_up, ffn_conv_w, ffn_conv_b, ffn_w_down, final_g):
    B, L, D = x.shape
    Lc = ctx.shape[1]
    depth = w_mod.shape[0]
    tm = min(256, L)
    tmc = min(256, Lc)
    tq = min(512, L)
    tk = min(1024, L)
    rope_lat = _rope_tables(L, True)
    rope_ctx = _rope_tables(Lc, False)
    mrows = 8 * ((B + 1 + 7) // 8)
    c_all = jnp.zeros((mrows, D), F32).at[:B].set(c).at[B].set(c_ctx)
    zeros_h = (jnp.zeros((S5_GROUPS, B, 2 * S5_STATE), F32),) * 2
    xc = ctx
    for i in range(depth):
        ctx_out = i < depth - 1
        mod = _modulation(c_all, w_mod[i], b_mod[i])
        m_lat = [m[:, None, :] for m in jnp.split(mod[:B], 6, axis=-1)]
        m_ctx = [jnp.broadcast_to(m[None, None, :], (B, 1, D)) for m in jnp.split(mod[B], 6, axis=-1)]
        wts = _layer_weights(w_in[i], mla_g_q[i], mla_w_uq[i], mla_g_kv[i], mla_w_ukv[i], s5_w_glu[i],
                             w_br_hy[i], w_br_s5[i], w_br_mla[i], w_o[i])
        s5_mats = _s5_matrices(s5_lam_re[i], s5_lam_im[i], s5_log_step[i], s5_b_re[i], s5_b_im[i],
                               s5_c_re[i], s5_c_im[i], S5_CHUNK)
        filt = (hy_f_w1[i], hy_f_b1[i], hy_f_w2[i], hy_f_b2[i], hy_f_w3[i], hy_f_freq[i], hy_f_decay[i])
        hy_args = (hy_conv_w[i], hy_conv_b[i], filt, hy_bias[i])
        n1g, n2g = norm1_g[i][None, None, :], norm2_g[i][None, None, :]
        w_up, w_down = ffn_w_up[i].astype(BF16), ffn_w_down[i].astype(BF16)

        xc_new, hc_fin, ctx_kv = _mixer(xc, m_ctx[0:3], n1g, wts, s5_mats, zeros_h, s5_d[i], hy_args,
                                        rope_ctx, None, ctx_out, tmc, tmc, tmc)
        x, _, _ = _mixer(x, m_lat[0:3], n1g, wts, s5_mats, hc_fin, s5_d[i], hy_args,
                         rope_lat, ctx_kv, True, tm, tq, tk)
        last = i == depth - 1
        x = _ffn(x, m_lat[3:6], n2g, w_up, ffn_conv_w[i], ffn_conv_b[i], w_down,
                 final_g if last else None, tm)
        if ctx_out:
            xc = _ffn(xc_new, m_ctx[3:6], n2g, w_up, ffn_conv_w[i], ffn_conv_b[i], w_down, None, tmc)
    return x
```

```python
import functools
import math

import jax
import jax.numpy as jnp
from jax import lax
from jax.experimental import pallas as pl
from jax.experimental.pallas import tpu as pltpu

F32 = jnp.float32
BF16 = jnp.bfloat16

EPS = 1e-6
GRID_W = 64
HY_WIDTH = 384
HY_ORDER = 2
HY_BANDS = 8
S5_WIDTH = 384
S5_GROUP = 16
S5_GROUPS = S5_WIDTH // S5_GROUP
S5_STATE = 64
S5_CHUNK = 128
MLA_HEADS = 8
MLA_NOPE = 64
MLA_ROPE = 32
MLA_V = 64
MLA_SCALE = (MLA_NOPE + MLA_ROPE) ** -0.5
LOG2E = math.log2(math.e)
ROPE_THETA = 10000.0
HEAD_PAD = 128
FFT_NB = 128
VMEM_LIMIT = 56 * 1024 * 1024


def _cparams(sem):
    return pltpu.CompilerParams(dimension_semantics=sem, vmem_limit_bytes=VMEM_LIMIT)


def _split(x):
    hi = x.astype(BF16)
    lo = (x - hi.astype(F32)).astype(BF16)
    return hi, lo


def _dot(a, b):
    return jnp.dot(a, b, preferred_element_type=F32)


def _dot3(ah, al, bh, bl):
    return _dot(ah, bh) + _dot(al, bh) + _dot(ah, bl)


def _sigmoid(x):
    return 1.0 / (1.0 + jnp.exp(-x))


def _silu(x):
    return x * _sigmoid(x)


def _rms(x, g):
    return x * lax.rsqrt(jnp.mean(x * x, axis=-1, keepdims=True) + EPS) * g


def _const_spec(shape):
    nd = len(shape)
    return pl.BlockSpec(shape, lambda *_: (0,) * nd)


def _mod_kernel(c_ref, w_ref, b_ref, o_ref):
    ch, cl = _split(_silu(c_ref[...]))
    wh, wl = _split(w_ref[...])
    o_ref[...] = _dot3(ch, cl, wh, wl) + b_ref[...]


def _modulation(c_all, w, b):
    m, d = c_all.shape
    n = w.shape[1]
    tn = 1024
    return pl.pallas_call(
        _mod_kernel,
        grid=(n // tn,),
        in_specs=[pl.BlockSpec((m, d), lambda j: (0, 0)),
                  pl.BlockSpec((d, tn), lambda j: (0, j)),
                  pl.BlockSpec((1, tn), lambda j: (0, j))],
        out_specs=pl.BlockSpec((m, tn), lambda j: (0, j)),
        out_shape=jax.ShapeDtypeStruct((m, n), F32),
        compiler_params=_cparams(("arbitrary",)),
    )(c_all, w, b.reshape(1, n))


def _rope(x, cos, sa, sb):
    return (x * cos + pltpu.roll(x, HEAD_PAD - MLA_ROPE // 2, 1) * sa
            + pltpu.roll(x, MLA_ROPE // 2, 1) * sb)


def _in_kernel(full, x_ref, ge_ref, sh_ref, cos_ref, sa_ref, sb_ref,
               w_s5, w_kv, w_kr, gkv_ref, w_uk, w_uv, *rest):
    if full:
        (w_q, gq_ref, w_uq, w_hy, w_gate,
         o_s5, o_k, o_v, o_q, o_hy, o_gate) = rest
    else:
        o_s5, o_k, o_v = rest
    hb = (_rms(x_ref[...], ge_ref[...]) + sh_ref[...]).astype(BF16)
    cos, sa, sb = cos_ref[...], sa_ref[...], sb_ref[...]
    o_s5[...] = _dot(hb, w_s5[...]).astype(o_s5.dtype)
    nkv = _rms(_dot(hb, w_kv[...]), gkv_ref[...]).astype(BF16)
    vfull = _dot(nkv, w_uv[...])
    knope = _dot(nkv, w_uk[...])
    kr = _rope(_dot(hb, w_kr[...]), cos, sa, sb)
    ones_col = (lax.broadcasted_iota(jnp.int32, (1, HEAD_PAD), 1) == MLA_V).astype(F32)
    for h in range(MLA_HEADS):
        o_k[h] = (knope[:, h * HEAD_PAD:(h + 1) * HEAD_PAD] + kr).astype(o_k.dtype)
        o_v[h] = (vfull[:, h * HEAD_PAD:(h + 1) * HEAD_PAD] + ones_col).astype(o_v.dtype)
    if full:
        nq = _rms(_dot(hb, w_q[...]), gq_ref[...]).astype(BF16)
        qf = _dot(nq, w_uq[...])
        for h in range(MLA_HEADS):
            qh = _rope(qf[:, h * HEAD_PAD:(h + 1) * HEAD_PAD], cos, sa, sb) * (MLA_SCALE * LOG2E)
            o_q[h] = qh.astype(o_q.dtype)
        o_hy[...] = _dot(hb, w_hy[...]).astype(o_hy.dtype)
        o_gate[...] = _dot(hb, w_gate[...]).astype(o_gate.dtype)


def _in_proj(x, ge, sh, rope_tabs, wts, full, tm):
    B, L, D = x.shape
    H = MLA_HEADS
    tok = lambda n: pl.BlockSpec((None, tm, n), lambda b, i: (b, i, 0))
    per_b = pl.BlockSpec((None, 1, D), lambda b, i: (b, 0, 0))
    tab = pl.BlockSpec((tm, HEAD_PAD), lambda b, i: (i, 0))
    head = pl.BlockSpec((None, H, tm, HEAD_PAD), lambda b, i: (b, 0, i, 0))
    names = ["w_s5", "w_kv", "w_kr", "g_kv", "w_uk", "w_uv"]
    if full:
        names += ["w_q", "g_q", "w_uq", "w_hy", "w_gate"]
    ws = [wts[n] for n in names]
    in_specs = [tok(D), per_b, per_b, tab, tab, tab] + [_const_spec(w.shape) for w in ws]
    out_specs = [tok(S5_WIDTH), head, head]
    out_shape = [jax.ShapeDtypeStruct((B, L, S5_WIDTH), BF16),
                 jax.ShapeDtypeStruct((B, H, L, HEAD_PAD), BF16),
                 jax.ShapeDtypeStruct((B, H, L, HEAD_PAD), BF16)]
    if full:
        n_hy = (HY_ORDER + 1) * HY_WIDTH
        out_specs += [head, tok(n_hy), tok(3 * D)]
        out_shape += [jax.ShapeDtypeStruct((B, H, L, HEAD_PAD), BF16),
                      jax.ShapeDtypeStruct((B, L, n_hy), BF16),
                      jax.ShapeDtypeStruct((B, L, 3 * D), BF16)]
    return pl.pallas_call(
        functools.partial(_in_kernel, full),
        grid=(B, L // tm),
        in_specs=in_specs, out_specs=out_specs, out_shape=out_shape,
        compiler_params=_cparams(("parallel", "parallel")),
    )(x, ge, sh, *rope_tabs, *ws)


def _attn_kernel(segs, tq, q_ref, *refs):
    o_ref = refs[-1]
    outs = []
    for hh in range(2):
        q = q_ref[hh]
        carry = (jnp.full((tq, 1), -jnp.inf, F32), jnp.zeros((tq, HEAD_PAD), F32))
        for si, (lk, tk) in enumerate(segs):
            k_ref, v_ref = refs[2 * si], refs[2 * si + 1]

            def body(j, c, k_ref=k_ref, v_ref=v_ref, tk=tk):
                m, acc = c
                off = pl.multiple_of(j * tk, tk)
                kb = k_ref[hh, pl.ds(off, tk), :]
                s = lax.dot_general(q, kb, (((1,), (1,)), ((), ())), preferred_element_type=F32)
                m_new = jnp.maximum(m, jnp.max(s, axis=-1, keepdims=True))
                p = jnp.exp2(s - m_new).astype(BF16)
                acc = jnp.exp2(m - m_new) * acc + _dot(p, v_ref[hh, pl.ds(off, tk), :])
                return m_new, acc

            carry = lax.fori_loop(0, lk // tk, body, carry)
        acc = carry[1]
        outs.append(acc[:, :MLA_V] / acc[:, MLA_V:MLA_V + 1])
    o_ref[...] = jnp.concatenate(outs, axis=1).astype(o_ref.dtype)


def _attention(q, kvs, tq, tks):
    B, H, Lq, _ = q.shape
    segs = tuple((k.shape[2], tk) for (k, _), tk in zip(kvs, tks))
    in_specs = [pl.BlockSpec((None, 2, tq, HEAD_PAD), lambda b, j, i: (b, j, i, 0))]
    args = [q]
    for k, v in kvs:
        lk = k.shape[2]
        in_specs += [pl.BlockSpec((None, 2, lk, HEAD_PAD), lambda b, j, i: (b, j, 0, 0))] * 2
        args += [k, v]
    return pl.pallas_call(
        functools.partial(_attn_kernel, segs, tq),
        grid=(B, H // 2, Lq // tq),
        in_specs=in_specs,
        out_specs=pl.BlockSpec((None, tq, 2 * MLA_V), lambda b, j, i: (b, i, j)),
        out_shape=jax.ShapeDtypeStruct((B, Lq, H * MLA_V), BF16),
        compiler_params=_cparams(("parallel", "parallel", "arbitrary")),
    )(*args)


def _s5_pack_kernel(cp, p_ref, o_ref):
    x = p_ref[...].astype(F32)
    for kk in range(cp):
        o_ref[kk] = x[kk * S5_CHUNK:(kk + 1) * S5_CHUNK, :].T


def _s5_pack(p):
    B, L, C = p.shape
    T = S5_CHUNK
    nch = L // T
    cp = min(8, nch)
    return pl.pallas_call(
        functools.partial(_s5_pack_kernel, cp),
        grid=(B, nch // cp),
        in_specs=[pl.BlockSpec((None, cp * T, C), lambda b, i: (b, i, 0))],
        out_specs=pl.BlockSpec((None, cp, C, T), lambda b, i: (b, i, 0, 0)),
        out_shape=jax.ShapeDtypeStruct((B, nch, C, T), F32),
        compiler_params=_cparams(("parallel", "parallel")),
    )(p)


def _s5_unpack_kernel(cp, z_ref, o_ref):
    for kk in range(cp):
        o_ref[kk * S5_CHUNK:(kk + 1) * S5_CHUNK, :] = z_ref[kk].T


def _s5_unpack(z):
    B, nch, C, T = z.shape
    cp = min(8, nch)
    return pl.pallas_call(
        functools.partial(_s5_unpack_kernel, cp),
        grid=(B, nch // cp),
        in_specs=[pl.BlockSpec((None, cp, C, T), lambda b, i: (b, i, 0, 0))],
        out_specs=pl.BlockSpec((None, cp * T, C), lambda b, i: (b, i, 0)),
        out_shape=jax.ShapeDtypeStruct((B, nch * T, C), F32),
        compiler_params=_cparams(("parallel", "parallel")),
    )(z)


def _toeplitz_kernel(w_ref, o_ref):
    T, cg = S5_CHUNK, S5_GROUP

    def body(cp, carry):
        r0 = pl.multiple_of(cp * T, T)
        for c in range(cg):
            w = w_ref[pl.ds(cp * cg + c, 1), :]
            blk = pltpu.roll(jnp.broadcast_to(w, (T, 2 * T)), 0, 1, stride=1, stride_axis=0)
            o_ref[pl.ds(r0, T), c * T:(c + 1) * T] = blk[:, :T].astype(BF16)
        return carry

    lax.fori_loop(0, cg, body, 0)


def _toeplitz(w):
    G, npair, W = w.shape
    n = S5_GROUP * S5_CHUNK
    return pl.pallas_call(
        _toeplitz_kernel,
        grid=(G,),
        in_specs=[pl.BlockSpec((None, npair, W), lambda g: (g, 0, 0))],
        out_specs=pl.BlockSpec((None, n, n), lambda g: (g, 0, 0)),
        out_shape=jax.ShapeDtypeStruct((G, n, n), BF16),
        compiler_params=_cparams(("parallel",)),
    )(w)


def _s5_kernel(nch, nb, u_ref, mi_ref, min_ref, mout_ref, atr_ref, ati_ref, h0r_ref, h0i_ref,
               y_ref, hfr_ref, hfi_ref, sr_ref, si_ref, hr_ref, hi_ref):
    ns, cg, T = S5_STATE, S5_GROUP, S5_CHUNK
    u = jnp.concatenate([u_ref[:, c, :] for c in range(cg)], axis=1).astype(BF16)
    s = _dot(u, min_ref[...])
    sr_ref[...] = s[:, :2 * ns]
    si_ref[...] = s[:, 2 * ns:]
    atr, ati = atr_ref[...], ati_ref[...]
    fwd = lax.broadcasted_iota(jnp.int32, (1, 2 * ns), 1) < ns

    def body(i, carry):
        out = []
        for b in range(nb):
            hr, hi = carry[2 * b], carry[2 * b + 1]
            rf = b * nch + i
            rb = b * nch + (nch - 1 - i)
            hr_ref[pl.ds(rf, 1), 0:ns] = hr[:, 0:ns]
            hi_ref[pl.ds(rf, 1), 0:ns] = hi[:, 0:ns]
            hr_ref[pl.ds(rb, 1), ns:2 * ns] = hr[:, ns:2 * ns]
            hi_ref[pl.ds(rb, 1), ns:2 * ns] = hi[:, ns:2 * ns]
            s_r = jnp.where(fwd, sr_ref[pl.ds(rf, 1), :], sr_ref[pl.ds(rb, 1), :])
            s_i = jnp.where(fwd, si_ref[pl.ds(rf, 1), :], si_ref[pl.ds(rb, 1), :])
            out += [atr * hr - ati * hi + s_r, atr * hi + ati * hr + s_i]
        return tuple(out)

    init = []
    for b in range(nb):
        init += [h0r_ref[b:b + 1, :], h0i_ref[b:b + 1, :]]
    fin = lax.fori_loop(0, nch, body, tuple(init))
    for b in range(nb):
        hfr_ref[b:b + 1, :] = fin[2 * b]
        hfi_ref[b:b + 1, :] = fin[2 * b + 1]
    hcat = jnp.concatenate([hr_ref[...], hi_ref[...]], axis=1).astype(BF16)
    cb = 4
    for j in range(cg // cb):
        cols = slice(j * cb * T, (j + 1) * cb * T)
        y = _dot(u, mi_ref[:, cols]) + _dot(hcat, mout_ref[:, cols])
        for c in range(cb):
            y_ref[:, j * cb + c, :] = y[:, c * T:(c + 1) * T]


def _cmul(ar, ai, br, bi):
    return ar * br - ai * bi, ar * bi + ai * br


def _s5_matrices(lam_re, lam_im, log_step, b_re, b_im, c_re, c_im):
    hp = lax.Precision.HIGHEST
    T = S5_CHUNK
    step = jnp.exp(log_step)[..., None]
    def apow(tau):
        mag = jnp.exp(lam_re[:, :, None, :] * step[:, :, None, :] * tau[None, None, :, None])
        ang = lam_im[:, :, None, :] * step[:, :, None, :] * tau[None, None, :, None]
        return mag * jnp.cos(ang), mag * jnp.sin(ang)
    a_r, a_i = apow(jnp.ones((1,), F32))
    a_r, a_i = a_r[:, :, 0], a_i[:, :, 0]
    den = lam_re ** 2 + lam_im ** 2
    q_r, q_i = _cmul(a_r - 1.0, a_i, lam_re / den, -lam_im / den)
    bb_r, bb_i = _cmul(q_r[..., None], q_i[..., None], b_re, b_im)
    tt = jnp.arange(T, dtype=F32)
    p_r, p_i = apow(tt)
    cb_r, cb_i = _cmul(jnp.swapaxes(c_re, -1, -2)[..., :, :, None], jnp.swapaxes(c_im, -1, -2)[..., :, :, None],
                       bb_r[..., :, None, :], bb_i[..., :, None, :])
    kk = (jnp.einsum('dgtn,dgncx->dgtcx', p_r, cb_r, precision=hp)
          - jnp.einsum('dgtn,dgncx->dgtcx', p_i, cb_i, precision=hp))
    kf, kb = kk[0], kk[1]
    G, cg = kf.shape[0], kf.shape[-1]
    wlag = jnp.concatenate([kf[:, :1] + kb[:, :1], kf[:, 1:], jnp.zeros_like(kf[:, :1]), kb[:, :0:-1]], axis=1)
    wlag = jnp.transpose(wlag, (0, 3, 2, 1)).reshape(G, cg * cg, 2 * T)
    m_intra = _toeplitz(wlag)
    def instate(pr, pi, br, bi):
        return _cmul(pr[:, None, :, :], pi[:, None, :, :],
                     jnp.swapaxes(br, -1, -2)[:, :, None, :], jnp.swapaxes(bi, -1, -2)[:, :, None, :])
    f_r, f_i = instate(p_r[0][:, ::-1], p_i[0][:, ::-1], bb_r[0], bb_i[0])
    g_r, g_i = instate(p_r[1], p_i[1], bb_r[1], bb_i[1])
    m_in = jnp.concatenate([f_r, g_r, f_i, g_i], axis=-1).reshape(G, cg * T, 4 * S5_STATE)
    p1_r, p1_i = apow(tt + 1.0)
    def outstate(pr, pi, cr, ci):
        xr, xi = _cmul(jnp.swapaxes(pr, 1, 2)[:, :, None, :], jnp.swapaxes(pi, 1, 2)[:, :, None, :],
                       jnp.swapaxes(cr, 1, 2)[:, :, :, None], jnp.swapaxes(ci, 1, 2)[:, :, :, None])
        return xr, -xi
    of_r, of_i = outstate(p1_r[0], p1_i[0], c_re[0], c_im[0])
    ob_r, ob_i = outstate(p1_r[1][:, ::-1], p1_i[1][:, ::-1], c_re[1], c_im[1])
    m_out = jnp.concatenate([of_r, ob_r, of_i, ob_i], axis=1).reshape(G, 4 * S5_STATE, cg * T)
    at_r, at_i = apow(jnp.full((1,), float(T), F32))
    at_r = jnp.concatenate([at_r[0, :, 0], at_r[1, :, 0]], axis=-1)[:, None, :]
    at_i = jnp.concatenate([at_i[0, :, 0], at_i[1, :, 0]], axis=-1)[:, None, :]
    return m_intra, m_in.astype(BF16), m_out.astype(BF16), at_r, at_i


def _s5_scan(u, mats, h0):
    B, L, C = u.shape
    T, G, cg, ns = S5_CHUNK, S5_GROUPS, S5_GROUP, S5_STATE
    nch = L // T
    R = nch * B
    m_intra, m_in, m_out, at_r, at_i = mats
    z = _s5_pack(u).reshape(R, C, T)
    gspec = lambda r, c: pl.BlockSpec((None, r, c), lambda g: (g, 0, 0))
    zspec = pl.BlockSpec((R, cg, T), lambda g: (0, g, 0))
    y, hfr, hfi = pl.pallas_call(
        functools.partial(_s5_kernel, nch, B),
        grid=(G,),
        in_specs=[zspec, gspec(T * cg, T * cg), gspec(T * cg, 4 * ns), gspec(4 * ns, T * cg),
                  gspec(1, 2 * ns), gspec(1, 2 * ns), gspec(B, 2 * ns), gspec(B, 2 * ns)],
        out_specs=[zspec, gspec(B, 2 * ns), gspec(B, 2 * ns)],
        out_shape=[jax.ShapeDtypeStruct((R, C, T), F32),
                   jax.ShapeDtypeStruct((G, B, 2 * ns), F32),
                   jax.ShapeDtypeStruct((G, B, 2 * ns), F32)],
        scratch_shapes=[pltpu.VMEM((R, 2 * ns), F32)] * 4,
        compiler_params=_cparams(("parallel",)),
    )(z, m_intra, m_in, m_out, at_r, at_i, h0[0], h0[1])
    return _s5_unpack(y.reshape(B, nch, C, T)), (hfr, hfi)


def _shift_rows(x, prev_row, next_row):
    n = x.shape[0]
    row = lax.broadcasted_iota(jnp.int32, x.shape, 0)
    xm = jnp.where(row == 0, prev_row, pltpu.roll(x, 1, 0))
    xp = jnp.where(row == n - 1, next_row, pltpu.roll(x, n - 1, 0))
    return xm, xp


def _dwconv_kernel(p_ref, w_ref, b_ref, o_ref):
    x = p_ref[...].astype(F32)
    w = w_ref[...]
    xm, xp = _shift_rows(x, 0.0, 0.0)
    o_ref[...] = xm * w[0:1] + x * w[1:2] + xp * w[2:3] + b_ref[...]


def _hy_dwconv(p, w, b):
    B, L, n = p.shape
    tc = 128
    nblk = HY_WIDTH // tc
    outs = []
    for part in range(n // HY_WIDTH):
        outs.append(pl.pallas_call(
            _dwconv_kernel,
            grid=(B, nblk),
            in_specs=[pl.BlockSpec((None, L, tc), lambda bb, j, part=part: (bb, 0, part * nblk + j)),
                      pl.BlockSpec((3, tc), lambda bb, j, part=part: (0, part * nblk + j)),
                      pl.BlockSpec((1, tc), lambda bb, j, part=part: (0, part * nblk + j))],
            out_specs=pl.BlockSpec((None, L, tc), lambda bb, j: (bb, 0, j)),
            out_shape=jax.ShapeDtypeStruct((B, L, HY_WIDTH), F32),
            compiler_params=_cparams(("parallel", "parallel")),
        )(p, w, b.reshape(1, n)))
    return outs


def _fft1_kernel(fh_ref, fl_ref, u_ref, o_ref):
    uh, ul = _split(u_ref[...])
    o_ref[...] = _dot3(fh_ref[...], fl_ref[...], uh, ul)


def _fft1(u2, f1):
    B, K, W = u2.shape
    M = f1[0].shape[0]
    tn = min(W, 6144)
    return pl.pallas_call(
        _fft1_kernel,
        grid=(B, W // tn),
        in_specs=[_const_spec((M, K)), _const_spec((M, K)),
                  pl.BlockSpec((None, K, tn), lambda b, j: (b, 0, j))],
        out_specs=pl.BlockSpec((None, M, tn), lambda b, j: (b, 0, j)),
        out_shape=jax.ShapeDtypeStruct((B, M, W), F32),
        compiler_params=_cparams(("parallel", "parallel")),
    )(f1[0], f1[1], u2)


def _fft2_kernel(conv, nbat, fh_ref, fl_ref, *rest):
    nb = FFT_NB
    if conv:
        gh_ref, gl_ref, h_ref, a_ref, o_ref = rest
        hsp = h_ref[...]
        hr, hi = hsp[:nb], hsp[nb:]
    else:
        a_ref, o_ref = rest
    for b in range(nbat):
        a = jnp.concatenate([a_ref[b, 0], a_ref[b, 1]], axis=0)
        ah, al = _split(a)
        x = _dot3(fh_ref[...], fl_ref[...], ah, al)
        if not conv:
            o_ref[b] = x
            continue
        xr, xi = x[:nb], x[nb:]
        y = jnp.concatenate([xr * hr - xi * hi, xr * hi + xi * hr], axis=0)
        yh, yl = _split(y)
        z = _dot3(gh_ref[...], gl_ref[...], yh, yl)
        o_ref[b, 0] = z[:nb]
        o_ref[b, 1] = z[nb:]


def _fft2_spectrum(a, fk):
    Bf, _, Na, Nb, C = a.shape
    mat = pl.BlockSpec((None, 2 * Nb, 2 * Nb), lambda k: (k, 0, 0))
    return pl.pallas_call(
        functools.partial(_fft2_kernel, False, Bf),
        grid=(Na,),
        in_specs=[mat, mat, pl.BlockSpec((Bf, 2, None, Nb, C), lambda k: (0, 0, k, 0, 0))],
        out_specs=pl.BlockSpec((Bf, None, 2 * Nb, C), lambda k: (0, k, 0, 0)),
        out_shape=jax.ShapeDtypeStruct((Bf, Na, 2 * Nb, C), F32),
        compiler_params=_cparams(("parallel",)),
    )(fk[0], fk[1], a)


def _fft2_conv(a, fk, gk, hspec):
    B, _, Na, Nb, C = a.shape
    mat = pl.BlockSpec((None, 2 * Nb, 2 * Nb), lambda k: (k, 0, 0))
    blk = pl.BlockSpec((B, 2, None, Nb, C), lambda k: (0, 0, k, 0, 0))
    return pl.pallas_call(
        functools.partial(_fft2_kernel, True, B),
        grid=(Na,),
        in_specs=[mat, mat, mat, mat, pl.BlockSpec((None, 2 * Nb, C), lambda k: (k, 0, 0)), blk],
        out_specs=blk,
        out_shape=jax.ShapeDtypeStruct(a.shape, F32),
        compiler_params=_cparams(("parallel",)),
    )(fk[0], fk[1], gk[0], gk[1], hspec, a)


def _fft3_kernel(fh_ref, fl_ref, b_ref, u_ref, bias_ref, x_ref, o_ref):
    bh, bl = _split(b_ref[...])
    y = _dot3(fh_ref[...], fl_ref[...], bh, bl)
    o_ref[...] = (x_ref[...] * (y + u_ref[...] * bias_ref[...])).astype(o_ref.dtype)


def _fft3(bmat, f3, u2, bias_t, x2):
    B, M2, W = bmat.shape
    K = u2.shape[1]
    tn = min(W, 6144)
    tok = pl.BlockSpec((None, K, tn), lambda b, j: (b, 0, j))
    return pl.pallas_call(
        _fft3_kernel,
        grid=(B, W // tn),
        in_specs=[_const_spec((K, M2)), _const_spec((K, M2)),
                  pl.BlockSpec((None, M2, tn), lambda b, j: (b, 0, j)), tok,
                  pl.BlockSpec((1, tn), lambda b, j: (0, j)), tok],
        out_specs=tok,
        out_shape=jax.ShapeDtypeStruct(u2.shape, F32),
        compiler_params=_cparams(("parallel", "parallel")),
    )(f3[0], f3[1], bmat, u2, bias_t, x2)


def _angles(rows, cols, n):
    prod = (rows[:, None] * cols[None, :]) % n
    return prod.astype(F32) * (2.0 * math.pi / n)


def _fft_consts(L):
    N = 2 * L
    Nb = FFT_NB
    Na = N // Nb
    ia = jnp.arange(Na, dtype=jnp.int32)
    ib = jnp.arange(Nb, dtype=jnp.int32)
    th1 = _angles(ia, ia, Na)
    f1 = jnp.concatenate([jnp.cos(th1), -jnp.sin(th1)], axis=0)
    kfull = ia[:, None] + Na * ib[None, :]
    th2 = ((kfull[:, :, None] * ib[None, None, :]) % N).astype(F32) * (2.0 * math.pi / N)
    c2, s2 = jnp.cos(th2), jnp.sin(th2)
    fk = jnp.concatenate([jnp.concatenate([c2, s2], axis=2),
                          jnp.concatenate([-s2, c2], axis=2)], axis=1)
    c2t, s2t = jnp.swapaxes(c2, 1, 2), jnp.swapaxes(s2, 1, 2)
    gk = jnp.concatenate([jnp.concatenate([c2t, -s2t], axis=2),
                          jnp.concatenate([s2t, c2t], axis=2)], axis=1)
    th3 = _angles(ia[:Na // 2], ia, Na)
    f3 = jnp.concatenate([jnp.cos(th3), -jnp.sin(th3)], axis=1) / N
    return dict(f1_half=_split(f1[:, :Na // 2]), f1_full=_split(f1), fk=_split(fk), gk=_split(gk),
                f3=_split(f3), Na=Na, Nb=Nb)


def _dense_kernel(conv, fh_ref, fl_ref, u_ref, *rest):
    u = u_ref[...]
    uh, ul = _split(u)
    x = _dot3(fh_ref[...], fl_ref[...], uh, ul)
    if not conv:
        rest[0][...] = x
        return
    gh_ref, gl_ref, h_ref, bias_ref, x_ref, o_ref = rest
    n = x.shape[0] // 2
    hsp = h_ref[...]
    xr, xi, hr, hi = x[:n], x[n:], hsp[:n], hsp[n:]
    y = jnp.concatenate([xr * hr - xi * hi, xr * hi + xi * hr], axis=0)
    yh, yl = _split(y)
    z = _dot3(gh_ref[...], gl_ref[...], yh, yl)
    o_ref[...] = (x_ref[...] * (z + u * bias_ref[...])).astype(o_ref.dtype)


def _dense_consts(L):
    N = 2 * L
    i_n = jnp.arange(N, dtype=jnp.int32)
    th = _angles(i_n, i_n, N)
    fd = jnp.concatenate([jnp.cos(th), -jnp.sin(th)], axis=0)
    tht = _angles(i_n[:L], i_n, N)
    fi = jnp.concatenate([jnp.cos(tht), -jnp.sin(tht)], axis=1) / N
    return dict(fd_half=_split(fd[:, :L]), fd_full=_split(fd), fi=_split(fi))


def _dense_spectrum(taps, fd):
    Bf, N, C = taps.shape
    return pl.pallas_call(
        functools.partial(_dense_kernel, False),
        grid=(Bf,),
        in_specs=[_const_spec(fd[0].shape), _const_spec(fd[0].shape),
                  pl.BlockSpec((None, N, C), lambda b: (b, 0, 0))],
        out_specs=pl.BlockSpec((None, 2 * N, C), lambda b: (b, 0, 0)),
        out_shape=jax.ShapeDtypeStruct((Bf, 2 * N, C), F32),
        compiler_params=_cparams(("parallel",)),
    )(fd[0], fd[1], taps)


def _dense_conv(u, fd, fi, hspec, bias, xg):
    B, L, C = u.shape
    tok = pl.BlockSpec((None, L, C), lambda b: (b, 0, 0))
    return pl.pallas_call(
        functools.partial(_dense_kernel, True),
        grid=(B,),
        in_specs=[_const_spec(fd[0].shape), _const_spec(fd[0].shape), tok,
                  _const_spec(fi[0].shape), _const_spec(fi[0].shape),
                  _const_spec(hspec.shape), _const_spec((1, C)), tok],
        out_specs=tok,
        out_shape=jax.ShapeDtypeStruct((B, L, C), F32),
        compiler_params=_cparams(("parallel",)),
    )(fd[0], fd[1], u, fi[0], fi[1], hspec, bias.reshape(1, C), xg)


def _hyena_filters(L, f_w1, f_b1, f_w2, f_b2, f_w3, f_freq, f_decay):
    hp = lax.Precision.HIGHEST
    t = jnp.arange(L, dtype=F32)[:, None]
    t_norm = t / L
    bands = jnp.arange(1, HY_BANDS + 1, dtype=F32)
    ang = (2.0 * math.pi / L) * t * bands
    feats = jnp.concatenate([t_norm, jnp.cos(ang), jnp.sin(ang)], axis=-1)
    z = jnp.sin(f_freq * (jnp.dot(feats, f_w1, precision=hp) + f_b1))
    z = jnp.sin(f_freq * (jnp.dot(z, f_w2, precision=hp) + f_b2))
    h = jnp.dot(z, f_w3, precision=hp) * jnp.exp(-t_norm * jnp.abs(f_decay))
    h = h.reshape(L, 2, HY_ORDER, HY_WIDTH)
    h = h / jnp.sum(jnp.abs(h), axis=(0, 1), keepdims=True)
    hf, hb = h[:, 0], h[:, 1]
    taps = jnp.concatenate([hf, jnp.zeros((1, HY_ORDER, HY_WIDTH), F32), hb[:0:-1]], axis=0)
    return jnp.transpose(taps, (1, 0, 2))


def _hyena(p_hy, conv_w, conv_b, filt, bias):
    B, L, _ = p_hy.shape
    C = HY_WIDTH
    taps = _hyena_filters(L, *filt)
    v, x1, x2 = _hy_dwconv(p_hy, conv_w, conv_b)
    gates = (x1, x2)
    u = v
    if 2 * L <= 1024:
        cst = _dense_consts(L)
        spec = _dense_spectrum(taps, cst["fd_full"])
        for o in range(HY_ORDER):
            u = _dense_conv(u, cst["fd_half"], cst["fi"], spec[o], bias[o], gates[o])
        return u
    cst = _fft_consts(L)
    Na, Nb = cst["Na"], cst["Nb"]
    W = Nb * C
    ta = _fft1(taps.reshape(HY_ORDER, Na, W), cst["f1_full"])
    spec = _fft2_spectrum(ta.reshape(HY_ORDER, 2, Na, Nb, C), cst["fk"])
    for o in range(HY_ORDER):
        u2 = u.reshape(B, Na // 2, W)
        a = _fft1(u2, cst["f1_half"])
        bm = _fft2_conv(a.reshape(B, 2, Na, Nb, C), cst["fk"], cst["gk"], spec[o])
        bias_t = jnp.tile(bias[o], Nb).reshape(1, W)
        u = _fft3(bm.reshape(B, 2 * Na, W), cst["f3"], u2, bias_t,
                  gates[o].reshape(B, Na // 2, W)).reshape(B, L, C)
    return u


def _gelu_tanh(x):
    return 0.5 * x * (1.0 + jnp.tanh(math.sqrt(2.0 / math.pi) * (x + 0.044715 * (x * x * x))))


def _merge_kernel(x_ref, g1_ref, pg_ref, yhy_ref, ys5_ref, ps5_ref, d_ref, ymla_ref,
                  wglu_ref, whyr, ws5r, wmlar, wo_ref, o_ref):
    D = x_ref.shape[-1]
    ys = _gelu_tanh(d_ref[...] * ps5_ref[...].astype(F32) + ys5_ref[...])
    ag = _dot(ys.astype(BF16), wglu_ref[...])
    y_s5 = ag[:, :S5_WIDTH] * _sigmoid(ag[:, S5_WIDTH:])
    pg = pg_ref[...].astype(F32)
    merged = (_sigmoid(pg[:, :D]) * _dot(yhy_ref[...].astype(BF16), whyr[...])
              + _sigmoid(pg[:, D:2 * D]) * _dot(y_s5.astype(BF16), ws5r[...])
              + _sigmoid(pg[:, 2 * D:]) * _dot(ymla_ref[...], wmlar[...]))
    o_ref[...] = x_ref[...] + g1_ref[...] * _dot(merged.astype(BF16), wo_ref[...])


def _merge(x, g1, pg, y_hy, ys5, ps5, d_skip, y_mla, wts, tm):
    B, L, D = x.shape
    tok = lambda n: pl.BlockSpec((None, tm, n), lambda b, i: (b, i, 0))
    per_b = pl.BlockSpec((None, 1, D), lambda b, i: (b, 0, 0))
    ws = [wts[n] for n in ("w_glu", "w_br_hy", "w_br_s5", "w_br_mla", "w_o")]
    return pl.pallas_call(
        _merge_kernel,
        grid=(B, L // tm),
        in_specs=[tok(D), per_b, tok(3 * D), tok(HY_WIDTH), tok(S5_WIDTH), tok(S5_WIDTH),
                  _const_spec((1, S5_WIDTH)), tok(MLA_HEADS * MLA_V)] + [_const_spec(w.shape) for w in ws],
        out_specs=tok(D),
        out_shape=jax.ShapeDtypeStruct((B, L, D), F32),
        compiler_params=_cparams(("parallel", "parallel")),
    )(x, g1, pg, y_hy, ys5, ps5, d_skip.reshape(1, -1), y_mla, *ws)


def _ffn_up_kernel(x_ref, ge_ref, sh_ref, w_ref, o_ref):
    hb = (_rms(x_ref[...], ge_ref[...]) + sh_ref[...]).astype(BF16)
    o_ref[...] = _dot(hb, w_ref[...]).astype(o_ref.dtype)


def _ffn_up(x, ge, sh, w_up, tm):
    B, L, D = x.shape
    n = w_up.shape[1]
    per_b = pl.BlockSpec((None, 1, D), lambda b, i: (b, 0, 0))
    return pl.pallas_call(
        _ffn_up_kernel,
        grid=(B, L // tm),
        in_specs=[pl.BlockSpec((None, tm, D), lambda b, i: (b, i, 0)), per_b, per_b, _const_spec(w_up.shape)],
        out_specs=pl.BlockSpec((None, tm, n), lambda b, i: (b, i, 0)),
        out_shape=jax.ShapeDtypeStruct((B, L, n), BF16),
        compiler_params=_cparams(("parallel", "parallel")),
    )(x, ge, sh, w_up)


def _ffn_down_kernel(final, nt, a_ref, ap_ref, an_ref, cw_ref, cb_ref, wd_ref, x_ref, g2_ref, *rest):
    i = pl.program_id(1)
    a = a_ref[...].astype(F32)
    hid = a.shape[1] // 2
    prev_row = jnp.where(i > 0, ap_ref[7:8, :].astype(F32), 0.0)
    next_row = jnp.where(i < nt - 1, an_ref[0:1, :].astype(F32), 0.0)
    am, ap = _shift_rows(a, prev_row, next_row)
    cw = cw_ref[...]
    c = am * cw[0:1] + a * cw[1:2] + ap * cw[2:3] + cb_ref[...]
    act = (_silu(c[:, :hid]) * c[:, hid:]).astype(BF16)
    y = x_ref[...] + g2_ref[...] * _dot(act, wd_ref[...])
    if final:
        fg_ref, o_ref = rest
        o_ref[...] = _rms(y, fg_ref[...])
    else:
        rest[0][...] = y


def _ffn_down(a, conv_w, conv_b, w_down, x, g2, final_g, tm):
    B, L, n = a.shape
    D = x.shape[-1]
    nt = L // tm
    r8 = tm // 8
    final = final_g is not None
    in_specs = [pl.BlockSpec((None, tm, n), lambda b, i: (b, i, 0)),
                pl.BlockSpec((None, 8, n), lambda b, i: (b, jnp.maximum(i * r8 - 1, 0), 0)),
                pl.BlockSpec((None, 8, n), lambda b, i: (b, jnp.minimum((i + 1) * r8, L // 8 - 1), 0)),
                _const_spec((3, n)), _const_spec((1, n)), _const_spec(w_down.shape),
                pl.BlockSpec((None, tm, D), lambda b, i: (b, i, 0)),
                pl.BlockSpec((None, 1, D), lambda b, i: (b, 0, 0))]
    args = [a, a, a, conv_w, conv_b.reshape(1, n), w_down, x, g2]
    if final:
        in_specs.append(_const_spec((1, D)))
        args.append(final_g.reshape(1, D))
    return pl.pallas_call(
        functools.partial(_ffn_down_kernel, final, nt),
        grid=(B, nt),
        in_specs=in_specs,
        out_specs=pl.BlockSpec((None, tm, D), lambda b, i: (b, i, 0)),
        out_shape=jax.ShapeDtypeStruct((B, L, D), F32),
        compiler_params=_cparams(("parallel", "parallel")),
    )(*args)


def _rope_tables(L, rope):
    half = MLA_ROPE // 2
    cos = jnp.ones((L, HEAD_PAD), F32)
    sa = jnp.zeros((L, HEAD_PAD), F32)
    sb = jnp.zeros((L, HEAD_PAD), F32)
    if not rope:
        return cos, sa, sb
    rows = L // GRID_W
    row = jnp.repeat(jnp.arange(rows, dtype=F32), GRID_W)
    col = jnp.tile(jnp.arange(GRID_W, dtype=F32), rows)
    n_ax = MLA_ROPE // 4
    inv = ROPE_THETA ** (-jnp.arange(n_ax, dtype=F32) / n_ax)
    ang = jnp.concatenate([row[:, None] * inv, col[:, None] * inv], axis=-1)
    c, s = jnp.cos(ang), jnp.sin(ang)
    cos = cos.at[:, MLA_NOPE:MLA_NOPE + half].set(c).at[:, MLA_NOPE + half:MLA_NOPE + 2 * half].set(c)
    sa = sa.at[:, MLA_NOPE:MLA_NOPE + half].set(-s)
    sb = sb.at[:, MLA_NOPE + half:MLA_NOPE + 2 * half].set(s)
    return cos, sa, sb


def _layer_weights(w_in, g_q, w_uq, g_kv, w_ukv, s5_w_glu, w_br_hy, w_br_s5, w_br_mla, w_o):
    D = w_in.shape[0]
    H = MLA_HEADS
    c_kv = S5_WIDTH
    c_kr = c_kv + w_ukv.shape[0]
    c_q = c_kr + MLA_ROPE
    c_hy = c_q + w_uq.shape[0]
    c_gate = c_hy + (HY_ORDER + 1) * HY_WIDTH
    bf = lambda a: a.astype(BF16)
    w_kr = jnp.zeros((D, HEAD_PAD), F32).at[:, MLA_NOPE:MLA_NOPE + MLA_ROPE].set(w_in[:, c_kr:c_q])
    ukv = w_ukv.reshape(-1, H, MLA_NOPE + MLA_V)
    w_uk = jnp.zeros((ukv.shape[0], H, HEAD_PAD), F32).at[:, :, :MLA_NOPE].set(ukv[:, :, :MLA_NOPE])
    w_uv = jnp.zeros((ukv.shape[0], H, HEAD_PAD), F32).at[:, :, :MLA_V].set(ukv[:, :, MLA_NOPE:])
    uq = w_uq.reshape(-1, H, MLA_NOPE + MLA_ROPE)
    w_uqp = jnp.zeros((uq.shape[0], H, HEAD_PAD), F32).at[:, :, :MLA_NOPE + MLA_ROPE].set(uq)
    return dict(
        w_s5=bf(w_in[:, :c_kv]), w_kv=bf(w_in[:, c_kv:c_kr]), w_kr=bf(w_kr), w_q=bf(w_in[:, c_q:c_hy]),
        w_hy=bf(w_in[:, c_hy:c_gate]), w_gate=bf(w_in[:, c_gate:]),
        g_kv=g_kv.reshape(1, -1), g_q=g_q.reshape(1, -1),
        w_uk=bf(w_uk.reshape(-1, H * HEAD_PAD)), w_uv=bf(w_uv.reshape(-1, H * HEAD_PAD)),
        w_uq=bf(w_uqp.reshape(-1, H * HEAD_PAD)),
        w_glu=bf(s5_w_glu), w_br_hy=bf(w_br_hy), w_br_s5=bf(w_br_s5), w_br_mla=bf(w_br_mla), w_o=bf(w_o))


def _mixer(x, mods, n1g, wts, s5_mats, s5_h0, s5_d, hy_args, rope_tabs, ctx_kv, full, tm, tq, tk):
    sh1, sc1, g1 = mods
    ge = n1g * (1.0 + sc1)
    outs = _in_proj(x, ge, sh1, rope_tabs, wts, full, tm)
    p_s5, k, v = outs[:3]
    y_s5, hfin = _s5_scan(p_s5, s5_mats, s5_h0)
    if not full:
        return None, hfin, (k, v)
    q, p_hy, p_gate = outs[3:]
    kvs = [(k, v)] + ([ctx_kv] if ctx_kv is not None else [])
    tks = [tk] + ([ctx_kv[0].shape[2]] if ctx_kv is not None else [])
    y_mla = _attention(q, kvs, tq, tks)
    y_hy = _hyena(p_hy, *hy_args)
    x_new = _merge(x, g1, p_gate, y_hy, y_s5, p_s5, s5_d, y_mla, wts, tm)
    return x_new, hfin, (k, v)


def _ffn(x, mods, n2g, w_up, conv_w, conv_b, w_down, final_g, tm):
    sh2, sc2, g2 = mods
    a = _ffn_up(x, n2g * (1.0 + sc2), sh2, w_up, tm)
    return _ffn_down(a, conv_w, conv_b, w_down, x, g2, final_g, tm)


def kernel(x, c, ctx, c_ctx, w_mod, b_mod, norm1_g, norm2_g, w_in, hy_conv_w, hy_conv_b, hy_f_w1, hy_f_b1, hy_f_w2, hy_f_b2, hy_f_w3, hy_f_freq, hy_f_decay, hy_bias, s5_lam_re, s5_lam_im, s5_log_step, s5_b_re, s5_b_im, s5_c_re, s5_c_im, s5_d, s5_w_glu, mla_g_q, mla_w_uq, mla_g_kv, mla_w_ukv, w_br_hy, w_br_s5, w_br_mla, w_o, ffn_w_up, ffn_conv_w, ffn_conv_b, ffn_w_down, final_g):
    B, L, D = x.shape
    Lc = ctx.shape[1]
    depth = w_mod.shape[0]
    tm = min(256, L)
    tmc = min(256, Lc)
    tq = min(512, L)
    tk = min(1024, L)
    rope_lat = _rope_tables(L, True)
    rope_ctx = _rope_tables(Lc, False)
    mrows = 8 * ((B + 1 + 7) // 8)
    c_all = jnp.zeros((mrows, D), F32).at[:B].set(c).at[B].set(c_ctx)
    zeros_h = (jnp.zeros((S5_GROUPS, B, 2 * S5_STATE), F32),) * 2
    xc = ctx
    for i in range(depth):
        ctx_out = i < depth - 1
        mod = _modulation(c_all, w_mod[i], b_mod[i])
        m_lat = [m[:, None, :] for m in jnp.split(mod[:B], 6, axis=-1)]
        m_ctx = [jnp.broadcast_to(m[None, None, :], (B, 1, D)) for m in jnp.split(mod[B], 6, axis=-1)]
        wts = _layer_weights(w_in[i], mla_g_q[i], mla_w_uq[i], mla_g_kv[i], mla_w_ukv[i], s5_w_glu[i],
                             w_br_hy[i], w_br_s5[i], w_br_mla[i], w_o[i])
        s5_mats = _s5_matrices(s5_lam_re[i], s5_lam_im[i], s5_log_step[i], s5_b_re[i], s5_b_im[i],
                               s5_c_re[i], s5_c_im[i])
        filt = (hy_f_w1[i], hy_f_b1[i], hy_f_w2[i], hy_f_b2[i], hy_f_w3[i], hy_f_freq[i], hy_f_decay[i])
        hy_args = (hy_conv_w[i], hy_conv_b[i], filt, hy_bias[i])
        n1g, n2g = norm1_g[i][None, None, :], norm2_g[i][None, None, :]
        w_up, w_down = ffn_w_up[i].astype(BF16), ffn_w_down[i].astype(BF16)

        xc_new, hc_fin, ctx_kv = _mixer(xc, m_ctx[0:3], n1g, wts, s5_mats, zeros_h, s5_d[i], hy_args,
                                        rope_ctx, None, ctx_out, tmc, tmc, tmc)
        x, _, _ = _mixer(x, m_lat[0:3], n1g, wts, s5_mats, hc_fin, s5_d[i], hy_args,
                         rope_lat, ctx_kv, True, tm, tq, tk)
        last = i == depth - 1
        x = _ffn(x, m_lat[3:6], n2g, w_up, ffn_conv_w[i], ffn_conv_b[i], w_down,
                 final_g if last else None, tm)
        if ctx_out:
            xc = _ffn(xc_new, m_ctx[3:6], n2g, w_up, ffn_conv_w[i], ffn_conv_b[i], w_down, None, tmc)
    return x
```

```python
import functools
import math

import jax
import jax.numpy as jnp
from jax import lax
from jax.experimental import pallas as pl
from jax.experimental.pallas import tpu as pltpu

F32 = jnp.float32
BF16 = jnp.bfloat16

EPS = 1e-6
GRID_W = 64
HY_WIDTH = 384
HY_ORDER = 2
HY_BANDS = 8
S5_WIDTH = 384
S5_GROUP = 16
S5_GROUPS = S5_WIDTH // S5_GROUP
S5_STATE = 64
S5_CHUNK = 128
MLA_HEADS = 8
MLA_NOPE = 64
MLA_ROPE = 32
MLA_V = 64
MLA_SCALE = (MLA_NOPE + MLA_ROPE) ** -0.5
LOG2E = math.log2(math.e)
ROPE_THETA = 10000.0
HEAD_PAD = 128
FFT_NB = 128
VMEM_LIMIT = 56 * 1024 * 1024


def _cparams(sem):
    return pltpu.CompilerParams(dimension_semantics=sem, vmem_limit_bytes=VMEM_LIMIT)


def _split(x):
    hi = x.astype(BF16)
    lo = (x - hi.astype(F32)).astype(BF16)
    return hi, lo


def _dot(a, b):
    return jnp.dot(a, b, preferred_element_type=F32)


def _dot3(ah, al, bh, bl):
    return _dot(ah, bh) + _dot(al, bh) + _dot(ah, bl)


def _sigmoid(x):
    return 1.0 / (1.0 + jnp.exp(-x))


def _silu(x):
    return x * _sigmoid(x)


def _rms(x, g):
    return x * lax.rsqrt(jnp.mean(x * x, axis=-1, keepdims=True) + EPS) * g


def _const_spec(shape):
    nd = len(shape)
    return pl.BlockSpec(shape, lambda *_: (0,) * nd)


def _mod_kernel(c_ref, w_ref, b_ref, o_ref):
    ch, cl = _split(_silu(c_ref[...]))
    wh, wl = _split(w_ref[...])
    o_ref[...] = _dot3(ch, cl, wh, wl) + b_ref[...]


def _modulation(c_all, w, b):
    m, d = c_all.shape
    n = w.shape[1]
    tn = 1024
    return pl.pallas_call(
        _mod_kernel,
        grid=(n // tn,),
        in_specs=[pl.BlockSpec((m, d), lambda j: (0, 0)),
                  pl.BlockSpec((d, tn), lambda j: (0, j)),
                  pl.BlockSpec((1, tn), lambda j: (0, j))],
        out_specs=pl.BlockSpec((m, tn), lambda j: (0, j)),
        out_shape=jax.ShapeDtypeStruct((m, n), F32),
        compiler_params=_cparams(("arbitrary",)),
    )(c_all, w, b.reshape(1, n))


def _rope(x, cos, sa, sb):
    return (x * cos + pltpu.roll(x, HEAD_PAD - MLA_ROPE // 2, 1) * sa
            + pltpu.roll(x, MLA_ROPE // 2, 1) * sb)


def _in_kernel(full, x_ref, ge_ref, sh_ref, cos_ref, sa_ref, sb_ref,
               w_s5, w_kv, w_kr, gkv_ref, w_uk, w_uv, *rest):
    if full:
        (w_q, gq_ref, w_uq, w_hy, w_gate,
         o_s5, o_k, o_v, o_q, o_hy, o_gate) = rest
    else:
        o_s5, o_k, o_v = rest
    hb = (_rms(x_ref[...], ge_ref[...]) + sh_ref[...]).astype(BF16)
    cos, sa, sb = cos_ref[...], sa_ref[...], sb_ref[...]
    o_s5[...] = _dot(hb, w_s5[...]).astype(o_s5.dtype)
    nkv = _rms(_dot(hb, w_kv[...]), gkv_ref[...]).astype(BF16)
    vfull = _dot(nkv, w_uv[...])
    knope = _dot(nkv, w_uk[...])
    kr = _rope(_dot(hb, w_kr[...]), cos, sa, sb)
    ones_col = (lax.broadcasted_iota(jnp.int32, (1, HEAD_PAD), 1) == MLA_V).astype(F32)
    for h in range(MLA_HEADS):
        o_k[h] = (knope[:, h * HEAD_PAD:(h + 1) * HEAD_PAD] + kr).astype(o_k.dtype)
        o_v[h] = (vfull[:, h * HEAD_PAD:(h + 1) * HEAD_PAD] + ones_col).astype(o_v.dtype)
    if full:
        nq = _rms(_dot(hb, w_q[...]), gq_ref[...]).astype(BF16)
        qf = _dot(nq, w_uq[...])
        for h in range(MLA_HEADS):
            qh = _rope(qf[:, h * HEAD_PAD:(h + 1) * HEAD_PAD], cos, sa, sb) * (MLA_SCALE * LOG2E)
            o_q[h] = qh.astype(o_q.dtype)
        o_hy[...] = _dot(hb, w_hy[...]).astype(o_hy.dtype)
        o_gate[...] = _dot(hb, w_gate[...]).astype(o_gate.dtype)


def _in_proj(x, ge, sh, rope_tabs, wts, full, tm):
    B, L, D = x.shape
    H = MLA_HEADS
    tok = lambda n: pl.BlockSpec((None, tm, n), lambda b, i: (b, i, 0))
    per_b = pl.BlockSpec((None, 1, D), lambda b, i: (b, 0, 0))
    tab = pl.BlockSpec((tm, HEAD_PAD), lambda b, i: (i, 0))
    head = pl.BlockSpec((None, H, tm, HEAD_PAD), lambda b, i: (b, 0, i, 0))
    names = ["w_s5", "w_kv", "w_kr", "g_kv", "w_uk", "w_uv"]
    if full:
        names += ["w_q", "g_q", "w_uq", "w_hy", "w_gate"]
    ws = [wts[n] for n in names]
    in_specs = [tok(D), per_b, per_b, tab, tab, tab] + [_const_spec(w.shape) for w in ws]
    out_specs = [tok(S5_WIDTH), head, head]
    out_shape = [jax.ShapeDtypeStruct((B, L, S5_WIDTH), BF16),
                 jax.ShapeDtypeStruct((B, H, L, HEAD_PAD), BF16),
                 jax.ShapeDtypeStruct((B, H, L, HEAD_PAD), BF16)]
    if full:
        n_hy = (HY_ORDER + 1) * HY_WIDTH
        out_specs += [head, tok(n_hy), tok(3 * D)]
        out_shape += [jax.ShapeDtypeStruct((B, H, L, HEAD_PAD), BF16),
                      jax.ShapeDtypeStruct((B, L, n_hy), BF16),
                      jax.ShapeDtypeStruct((B, L, 3 * D), BF16)]
    return pl.pallas_call(
        functools.partial(_in_kernel, full),
        grid=(B, L // tm),
        in_specs=in_specs, out_specs=out_specs, out_shape=out_shape,
        compiler_params=_cparams(("parallel", "parallel")),
    )(x, ge, sh, *rope_tabs, *ws)


def _attn_kernel(segs, tq, q_ref, *refs):
    o_ref = refs[-1]
    nh = 2
    qs = [q_ref[hh] for hh in range(nh)]
    carry = tuple((jnp.full((tq, 1), -jnp.inf, F32), jnp.zeros((tq, HEAD_PAD), F32)) for _ in range(nh))
    for si, (lk, tk) in enumerate(segs):
        k_ref, v_ref = refs[2 * si], refs[2 * si + 1]

        def body(j, c, k_ref=k_ref, v_ref=v_ref, tk=tk):
            off = pl.multiple_of(j * tk, tk)
            out = []
            for hh in range(nh):
                m, acc = c[hh]
                kb = k_ref[hh, pl.ds(off, tk), :]
                s = lax.dot_general(qs[hh], kb, (((1,), (1,)), ((), ())), preferred_element_type=F32)
                m_new = jnp.maximum(m, jnp.max(s, axis=-1, keepdims=True))
                p = jnp.exp2(s - m_new).astype(BF16)
                acc = jnp.exp2(m - m_new) * acc + _dot(p, v_ref[hh, pl.ds(off, tk), :])
                out.append((m_new, acc))
            return tuple(out)

        carry = lax.fori_loop(0, lk // tk, body, carry)
    outs = [acc[:, :MLA_V] / acc[:, MLA_V:MLA_V + 1] for _, acc in carry]
    o_ref[...] = jnp.concatenate(outs, axis=1).astype(o_ref.dtype)


def _attention(q, kvs, tq, tks):
    B, H, Lq, _ = q.shape
    segs = tuple((k.shape[2], tk) for (k, _), tk in zip(kvs, tks))
    in_specs = [pl.BlockSpec((None, 2, tq, HEAD_PAD), lambda b, j, i: (b, j, i, 0))]
    args = [q]
    for k, v in kvs:
        lk = k.shape[2]
        in_specs += [pl.BlockSpec((None, 2, lk, HEAD_PAD), lambda b, j, i: (b, j, 0, 0))] * 2
        args += [k, v]
    return pl.pallas_call(
        functools.partial(_attn_kernel, segs, tq),
        grid=(B, H // 2, Lq // tq),
        in_specs=in_specs,
        out_specs=pl.BlockSpec((None, tq, 2 * MLA_V), lambda b, j, i: (b, i, j)),
        out_shape=jax.ShapeDtypeStruct((B, Lq, H * MLA_V), BF16),
        compiler_params=_cparams(("parallel", "parallel", "arbitrary")),
    )(*args)


def _s5_pack_kernel(cp, p_ref, o_ref):
    x = p_ref[...].astype(F32)
    for kk in range(cp):
        o_ref[kk] = x[kk * S5_CHUNK:(kk + 1) * S5_CHUNK, :].T


def _s5_pack(p):
    B, L, C = p.shape
    T = S5_CHUNK
    nch = L // T
    cp = min(8, nch)
    return pl.pallas_call(
        functools.partial(_s5_pack_kernel, cp),
        grid=(B, nch // cp),
        in_specs=[pl.BlockSpec((None, cp * T, C), lambda b, i: (b, i, 0))],
        out_specs=pl.BlockSpec((None, cp, C, T), lambda b, i: (b, i, 0, 0)),
        out_shape=jax.ShapeDtypeStruct((B, nch, C, T), F32),
        compiler_params=_cparams(("parallel", "parallel")),
    )(p)


def _s5_unpack_kernel(cp, z_ref, o_ref):
    for kk in range(cp):
        o_ref[kk * S5_CHUNK:(kk + 1) * S5_CHUNK, :] = z_ref[kk].T


def _s5_unpack(z):
    B, nch, C, T = z.shape
    cp = min(8, nch)
    return pl.pallas_call(
        functools.partial(_s5_unpack_kernel, cp),
        grid=(B, nch // cp),
        in_specs=[pl.BlockSpec((None, cp, C, T), lambda b, i: (b, i, 0, 0))],
        out_specs=pl.BlockSpec((None, cp * T, C), lambda b, i: (b, i, 0)),
        out_shape=jax.ShapeDtypeStruct((B, nch * T, C), F32),
        compiler_params=_cparams(("parallel", "parallel")),
    )(z)


def _toeplitz_kernel(w_ref, o_ref):
    T, cg = S5_CHUNK, S5_GROUP

    def body(cp, carry):
        r0 = pl.multiple_of(cp * T, T)
        for c in range(cg):
            w = w_ref[pl.ds(cp * cg + c, 1), :]
            blk = pltpu.roll(jnp.broadcast_to(w, (T, 2 * T)), 0, 1, stride=1, stride_axis=0)
            o_ref[pl.ds(r0, T), c * T:(c + 1) * T] = blk[:, :T].astype(BF16)
        return carry

    lax.fori_loop(0, cg, body, 0)


def _toeplitz(w):
    G, npair, W = w.shape
    n = S5_GROUP * S5_CHUNK
    return pl.pallas_call(
        _toeplitz_kernel,
        grid=(G,),
        in_specs=[pl.BlockSpec((None, npair, W), lambda g: (g, 0, 0))],
        out_specs=pl.BlockSpec((None, n, n), lambda g: (g, 0, 0)),
        out_shape=jax.ShapeDtypeStruct((G, n, n), BF16),
        compiler_params=_cparams(("parallel",)),
    )(w)


def _s5_kernel(nch, nb, u_ref, mi_ref, min_ref, mout_ref, atr_ref, ati_ref, h0r_ref, h0i_ref,
               y_ref, hfr_ref, hfi_ref, sr_ref, si_ref, hr_ref, hi_ref):
    ns, cg, T = S5_STATE, S5_GROUP, S5_CHUNK
    u = jnp.concatenate([u_ref[:, c, :] for c in range(cg)], axis=1).astype(BF16)
    s = _dot(u, min_ref[...])
    sr_ref[...] = s[:, :2 * ns]
    si_ref[...] = s[:, 2 * ns:]
    atr, ati = atr_ref[...], ati_ref[...]
    fwd = lax.broadcasted_iota(jnp.int32, (1, 2 * ns), 1) < ns

    def body(i, carry):
        out = []
        for b in range(nb):
            hr, hi = carry[2 * b], carry[2 * b + 1]
            rf = b * nch + i
            rb = b * nch + (nch - 1 - i)
            hr_ref[pl.ds(rf, 1), 0:ns] = hr[:, 0:ns]
            hi_ref[pl.ds(rf, 1), 0:ns] = hi[:, 0:ns]
            hr_ref[pl.ds(rb, 1), ns:2 * ns] = hr[:, ns:2 * ns]
            hi_ref[pl.ds(rb, 1), ns:2 * ns] = hi[:, ns:2 * ns]
            s_r = jnp.where(fwd, sr_ref[pl.ds(rf, 1), :], sr_ref[pl.ds(rb, 1), :])
            s_i = jnp.where(fwd, si_ref[pl.ds(rf, 1), :], si_ref[pl.ds(rb, 1), :])
            out += [atr * hr - ati * hi + s_r, atr * hi + ati * hr + s_i]
        return tuple(out)

    init = []
    for b in range(nb):
        init += [h0r_ref[b:b + 1, :], h0i_ref[b:b + 1, :]]
    fin = lax.fori_loop(0, nch, body, tuple(init))
    for b in range(nb):
        hfr_ref[b:b + 1, :] = fin[2 * b]
        hfi_ref[b:b + 1, :] = fin[2 * b + 1]
    hcat = jnp.concatenate([hr_ref[...], hi_ref[...]], axis=1).astype(BF16)
    cb = 4
    for j in range(cg // cb):
        cols = slice(j * cb * T, (j + 1) * cb * T)
        y = _dot(u, mi_ref[:, cols]) + _dot(hcat, mout_ref[:, cols])
        for c in range(cb):
            y_ref[:, j * cb + c, :] = y[:, c * T:(c + 1) * T]


def _cmul(ar, ai, br, bi):
    return ar * br - ai * bi, ar * bi + ai * br


def _s5_matrices(lam_re, lam_im, log_step, b_re, b_im, c_re, c_im):
    hp = lax.Precision.HIGHEST
    T = S5_CHUNK
    step = jnp.exp(log_step)[..., None]
    def apow(tau):
        mag = jnp.exp(lam_re[:, :, None, :] * step[:, :, None, :] * tau[None, None, :, None])
        ang = lam_im[:, :, None, :] * step[:, :, None, :] * tau[None, None, :, None]
        return mag * jnp.cos(ang), mag * jnp.sin(ang)
    a_r, a_i = apow(jnp.ones((1,), F32))
    a_r, a_i = a_r[:, :, 0], a_i[:, :, 0]
    den = lam_re ** 2 + lam_im ** 2
    q_r, q_i = _cmul(a_r - 1.0, a_i, lam_re / den, -lam_im / den)
    bb_r, bb_i = _cmul(q_r[..., None], q_i[..., None], b_re, b_im)
    tt = jnp.arange(T, dtype=F32)
    p_r, p_i = apow(tt)
    cb_r, cb_i = _cmul(jnp.swapaxes(c_re, -1, -2)[..., :, :, None], jnp.swapaxes(c_im, -1, -2)[..., :, :, None],
                       bb_r[..., :, None, :], bb_i[..., :, None, :])
    kk = (jnp.einsum('dgtn,dgncx->dgtcx', p_r, cb_r, precision=hp)
          - jnp.einsum('dgtn,dgncx->dgtcx', p_i, cb_i, precision=hp))
    kf, kb = kk[0], kk[1]
    G, cg = kf.shape[0], kf.shape[-1]
    wlag = jnp.concatenate([kf[:, :1] + kb[:, :1], kf[:, 1:], jnp.zeros_like(kf[:, :1]), kb[:, :0:-1]], axis=1)
    wlag = jnp.transpose(wlag, (0, 3, 2, 1)).reshape(G, cg * cg, 2 * T)
    m_intra = _toeplitz(wlag)
    def instate(pr, pi, br, bi):
        return _cmul(pr[:, None, :, :], pi[:, None, :, :],
                     jnp.swapaxes(br, -1, -2)[:, :, None, :], jnp.swapaxes(bi, -1, -2)[:, :, None, :])
    f_r, f_i = instate(p_r[0][:, ::-1], p_i[0][:, ::-1], bb_r[0], bb_i[0])
    g_r, g_i = instate(p_r[1], p_i[1], bb_r[1], bb_i[1])
    m_in = jnp.concatenate([f_r, g_r, f_i, g_i], axis=-1).reshape(G, cg * T, 4 * S5_STATE)
    p1_r, p1_i = apow(tt + 1.0)
    def outstate(pr, pi, cr, ci):
        xr, xi = _cmul(jnp.swapaxes(pr, 1, 2)[:, :, None, :], jnp.swapaxes(pi, 1, 2)[:, :, None, :],
                       jnp.swapaxes(cr, 1, 2)[:, :, :, None], jnp.swapaxes(ci, 1, 2)[:, :, :, None])
        return xr, -xi
    of_r, of_i = outstate(p1_r[0], p1_i[0], c_re[0], c_im[0])
    ob_r, ob_i = outstate(p1_r[1][:, ::-1], p1_i[1][:, ::-1], c_re[1], c_im[1])
    m_out = jnp.concatenate([of_r, ob_r, of_i, ob_i], axis=1).reshape(G, 4 * S5_STATE, cg * T)
    at_r, at_i = apow(jnp.full((1,), float(T), F32))
    at_r = jnp.concatenate([at_r[0, :, 0], at_r[1, :, 0]], axis=-1)[:, None, :]
    at_i = jnp.concatenate([at_i[0, :, 0], at_i[1, :, 0]], axis=-1)[:, None, :]
    return m_intra, m_in.astype(BF16), m_out.astype(BF16), at_r, at_i


def _s5_scan(u, mats, h0):
    B, L, C = u.shape
    T, G, cg, ns = S5_CHUNK, S5_GROUPS, S5_GROUP, S5_STATE
    nch = L // T
    R = nch * B
    m_intra, m_in, m_out, at_r, at_i = mats
    z = _s5_pack(u).reshape(R, C, T)
    gspec = lambda r, c: pl.BlockSpec((None, r, c), lambda g: (g, 0, 0))
    zspec = pl.BlockSpec((R, cg, T), lambda g: (0, g, 0))
    y, hfr, hfi = pl.pallas_call(
        functools.partial(_s5_kernel, nch, B),
        grid=(G,),
        in_specs=[zspec, gspec(T * cg, T * cg), gspec(T * cg, 4 * ns), gspec(4 * ns, T * cg),
                  gspec(1, 2 * ns), gspec(1, 2 * ns), gspec(B, 2 * ns), gspec(B, 2 * ns)],
        out_specs=[zspec, gspec(B, 2 * ns), gspec(B, 2 * ns)],
        out_shape=[jax.ShapeDtypeStruct((R, C, T), F32),
                   jax.ShapeDtypeStruct((G, B, 2 * ns), F32),
                   jax.ShapeDtypeStruct((G, B, 2 * ns), F32)],
        scratch_shapes=[pltpu.VMEM((R, 2 * ns), F32)] * 4,
        compiler_params=_cparams(("parallel",)),
    )(z, m_intra, m_in, m_out, at_r, at_i, h0[0], h0[1])
    return _s5_unpack(y.reshape(B, nch, C, T)), (hfr, hfi)


def _shift_rows(x, prev_row, next_row):
    n = x.shape[0]
    row = lax.broadcasted_iota(jnp.int32, x.shape, 0)
    xm = jnp.where(row == 0, prev_row, pltpu.roll(x, 1, 0))
    xp = jnp.where(row == n - 1, next_row, pltpu.roll(x, n - 1, 0))
    return xm, xp


def _dwconv_kernel(p_ref, w_ref, b_ref, o_ref):
    x = p_ref[...].astype(F32)
    w = w_ref[...]
    xm, xp = _shift_rows(x, 0.0, 0.0)
    o_ref[...] = xm * w[0:1] + x * w[1:2] + xp * w[2:3] + b_ref[...]


def _hy_dwconv(p, w, b):
    B, L, n = p.shape
    tc = 128
    nblk = HY_WIDTH // tc
    outs = []
    for part in range(n // HY_WIDTH):
        outs.append(pl.pallas_call(
            _dwconv_kernel,
            grid=(B, nblk),
            in_specs=[pl.BlockSpec((None, L, tc), lambda bb, j, part=part: (bb, 0, part * nblk + j)),
                      pl.BlockSpec((3, tc), lambda bb, j, part=part: (0, part * nblk + j)),
                      pl.BlockSpec((1, tc), lambda bb, j, part=part: (0, part * nblk + j))],
            out_specs=pl.BlockSpec((None, L, tc), lambda bb, j: (bb, 0, j)),
            out_shape=jax.ShapeDtypeStruct((B, L, HY_WIDTH), F32),
            compiler_params=_cparams(("parallel", "parallel")),
        )(p, w, b.reshape(1, n)))
    return outs


def _fft1_kernel(fh_ref, fl_ref, u_ref, o_ref):
    uh, ul = _split(u_ref[...])
    o_ref[...] = _dot3(fh_ref[...], fl_ref[...], uh, ul)


def _fft1(u2, f1):
    B, K, W = u2.shape
    M = f1[0].shape[0]
    tn = min(W, 6144)
    return pl.pallas_call(
        _fft1_kernel,
        grid=(B, W // tn),
        in_specs=[_const_spec((M, K)), _const_spec((M, K)),
                  pl.BlockSpec((None, K, tn), lambda b, j: (b, 0, j))],
        out_specs=pl.BlockSpec((None, M, tn), lambda b, j: (b, 0, j)),
        out_shape=jax.ShapeDtypeStruct((B, M, W), F32),
        compiler_params=_cparams(("parallel", "parallel")),
    )(f1[0], f1[1], u2)


def _fft2_kernel(conv, nbat, fh_ref, fl_ref, *rest):
    nb = FFT_NB
    if conv:
        gh_ref, gl_ref, h_ref, a_ref, o_ref = rest
        hsp = h_ref[...]
        hr, hi = hsp[:nb], hsp[nb:]
    else:
        a_ref, o_ref = rest
    for b in range(nbat):
        a = jnp.concatenate([a_ref[b, 0], a_ref[b, 1]], axis=0)
        ah, al = _split(a)
        x = _dot3(fh_ref[...], fl_ref[...], ah, al)
        if not conv:
            o_ref[b] = x
            continue
        xr, xi = x[:nb], x[nb:]
        y = jnp.concatenate([xr * hr - xi * hi, xr * hi + xi * hr], axis=0)
        yh, yl = _split(y)
        z = _dot3(gh_ref[...], gl_ref[...], yh, yl)
        o_ref[b, 0] = z[:nb]
        o_ref[b, 1] = z[nb:]


def _fft2_spectrum(a, fk):
    Bf, _, Na, Nb, C = a.shape
    mat = pl.BlockSpec((None, 2 * Nb, 2 * Nb), lambda k: (k, 0, 0))
    return pl.pallas_call(
        functools.partial(_fft2_kernel, False, Bf),
        grid=(Na,),
        in_specs=[mat, mat, pl.BlockSpec((Bf, 2, None, Nb, C), lambda k: (0, 0, k, 0, 0))],
        out_specs=pl.BlockSpec((Bf, None, 2 * Nb, C), lambda k: (0, k, 0, 0)),
        out_shape=jax.ShapeDtypeStruct((Bf, Na, 2 * Nb, C), F32),
        compiler_params=_cparams(("parallel",)),
    )(fk[0], fk[1], a)


def _fft2_conv(a, fk, gk, hspec):
    B, _, Na, Nb, C = a.shape
    mat = pl.BlockSpec((None, 2 * Nb, 2 * Nb), lambda k: (k, 0, 0))
    blk = pl.BlockSpec((B, 2, None, Nb, C), lambda k: (0, 0, k, 0, 0))
    return pl.pallas_call(
        functools.partial(_fft2_kernel, True, B),
        grid=(Na,),
        in_specs=[mat, mat, mat, mat, pl.BlockSpec((None, 2 * Nb, C), lambda k: (k, 0, 0)), blk],
        out_specs=blk,
        out_shape=jax.ShapeDtypeStruct(a.shape, F32),
        compiler_params=_cparams(("parallel",)),
    )(fk[0], fk[1], gk[0], gk[1], hspec, a)


def _fft3_kernel(fh_ref, fl_ref, b_ref, u_ref, bias_ref, x_ref, o_ref):
    bh, bl = _split(b_ref[...])
    y = _dot3(fh_ref[...], fl_ref[...], bh, bl)
    o_ref[...] = (x_ref[...] * (y + u_ref[...] * bias_ref[...])).astype(o_ref.dtype)


def _fft3(bmat, f3, u2, bias_t, x2):
    B, M2, W = bmat.shape
    K = u2.shape[1]
    tn = min(W, 6144)
    tok = pl.BlockSpec((None, K, tn), lambda b, j: (b, 0, j))
    return pl.pallas_call(
        _fft3_kernel,
        grid=(B, W // tn),
        in_specs=[_const_spec((K, M2)), _const_spec((K, M2)),
                  pl.BlockSpec((None, M2, tn), lambda b, j: (b, 0, j)), tok,
                  pl.BlockSpec((1, tn), lambda b, j: (0, j)), tok],
        out_specs=tok,
        out_shape=jax.ShapeDtypeStruct(u2.shape, F32),
        compiler_params=_cparams(("parallel", "parallel")),
    )(f3[0], f3[1], bmat, u2, bias_t, x2)


def _angles(rows, cols, n):
    prod = (rows[:, None] * cols[None, :]) % n
    return prod.astype(F32) * (2.0 * math.pi / n)


def _fft_consts(L):
    N = 2 * L
    Nb = FFT_NB
    Na = N // Nb
    nk = Na // 2 + 1
    nkp = 8 * ((nk + 7) // 8)
    ia = jnp.arange(Na, dtype=jnp.int32)
    ib = jnp.arange(Nb, dtype=jnp.int32)
    ik = jnp.arange(nkp, dtype=jnp.int32)
    valid = (ik < nk).astype(F32)
    th1 = _angles(ik, ia, Na)
    f1 = jnp.concatenate([jnp.cos(th1) * valid[:, None], -jnp.sin(th1) * valid[:, None]], axis=0)
    kfull = ik[:, None] + Na * ib[None, :]
    th2 = ((kfull[:, :, None] * ib[None, None, :]) % N).astype(F32) * (2.0 * math.pi / N)
    c2, s2 = jnp.cos(th2), jnp.sin(th2)
    fk = jnp.concatenate([jnp.concatenate([c2, s2], axis=2),
                          jnp.concatenate([-s2, c2], axis=2)], axis=1)
    c2t, s2t = jnp.swapaxes(c2, 1, 2), jnp.swapaxes(s2, 1, 2)
    gk = jnp.concatenate([jnp.concatenate([c2t, -s2t], axis=2),
                          jnp.concatenate([s2t, c2t], axis=2)], axis=1)
    th3 = _angles(ia[:Na // 2], ik, Na)
    wgt = jnp.where((ik == 0) | (ik == Na // 2), 1.0, 2.0) * valid / N
    f3 = jnp.concatenate([jnp.cos(th3) * wgt[None, :], -jnp.sin(th3) * wgt[None, :]], axis=1)
    return dict(f1_half=_split(f1[:, :Na // 2]), f1_full=_split(f1), fk=_split(fk), gk=_split(gk),
                f3=_split(f3), Na=Na, Nb=Nb, nkp=nkp)


def _dense_kernel(conv, fh_ref, fl_ref, u_ref, *rest):
    u = u_ref[...]
    uh, ul = _split(u)
    x = _dot3(fh_ref[...], fl_ref[...], uh, ul)
    if not conv:
        rest[0][...] = x
        return
    gh_ref, gl_ref, h_ref, bias_ref, x_ref, o_ref = rest
    n = x.shape[0] // 2
    hsp = h_ref[...]
    xr, xi, hr, hi = x[:n], x[n:], hsp[:n], hsp[n:]
    y = jnp.concatenate([xr * hr - xi * hi, xr * hi + xi * hr], axis=0)
    yh, yl = _split(y)
    z = _dot3(gh_ref[...], gl_ref[...], yh, yl)
    o_ref[...] = (x_ref[...] * (z + u * bias_ref[...])).astype(o_ref.dtype)


def _dense_consts(L):
    N = 2 * L
    i_n = jnp.arange(N, dtype=jnp.int32)
    th = _angles(i_n, i_n, N)
    fd = jnp.concatenate([jnp.cos(th), -jnp.sin(th)], axis=0)
    tht = _angles(i_n[:L], i_n, N)
    fi = jnp.concatenate([jnp.cos(tht), -jnp.sin(tht)], axis=1) / N
    return dict(fd_half=_split(fd[:, :L]), fd_full=_split(fd), fi=_split(fi))


def _dense_spectrum(taps, fd):
    Bf, N, C = taps.shape
    return pl.pallas_call(
        functools.partial(_dense_kernel, False),
        grid=(Bf,),
        in_specs=[_const_spec(fd[0].shape), _const_spec(fd[0].shape),
                  pl.BlockSpec((None, N, C), lambda b: (b, 0, 0))],
        out_specs=pl.BlockSpec((None, 2 * N, C), lambda b: (b, 0, 0)),
        out_shape=jax.ShapeDtypeStruct((Bf, 2 * N, C), F32),
        compiler_params=_cparams(("parallel",)),
    )(fd[0], fd[1], taps)


def _dense_conv(u, fd, fi, hspec, bias, xg):
    B, L, C = u.shape
    tok = pl.BlockSpec((None, L, C), lambda b: (b, 0, 0))
    return pl.pallas_call(
        functools.partial(_dense_kernel, True),
        grid=(B,),
        in_specs=[_const_spec(fd[0].shape), _const_spec(fd[0].shape), tok,
                  _const_spec(fi[0].shape), _const_spec(fi[0].shape),
                  _const_spec(hspec.shape), _const_spec((1, C)), tok],
        out_specs=tok,
        out_shape=jax.ShapeDtypeStruct((B, L, C), F32),
        compiler_params=_cparams(("parallel",)),
    )(fd[0], fd[1], u, fi[0], fi[1], hspec, bias.reshape(1, C), xg)


def _hyena_filters(L, f_w1, f_b1, f_w2, f_b2, f_w3, f_freq, f_decay):
    hp = lax.Precision.HIGHEST
    t = jnp.arange(L, dtype=F32)[:, None]
    t_norm = t / L
    bands = jnp.arange(1, HY_BANDS + 1, dtype=F32)
    ang = (2.0 * math.pi / L) * t * bands
    feats = jnp.concatenate([t_norm, jnp.cos(ang), jnp.sin(ang)], axis=-1)
    z = jnp.sin(f_freq * (jnp.dot(feats, f_w1, precision=hp) + f_b1))
    z = jnp.sin(f_freq * (jnp.dot(z, f_w2, precision=hp) + f_b2))
    h = jnp.dot(z, f_w3, precision=hp) * jnp.exp(-t_norm * jnp.abs(f_decay))
    h = h.reshape(L, 2, HY_ORDER, HY_WIDTH)
    h = h / jnp.sum(jnp.abs(h), axis=(0, 1), keepdims=True)
    hf, hb = h[:, 0], h[:, 1]
    taps = jnp.concatenate([hf, jnp.zeros((1, HY_ORDER, HY_WIDTH), F32), hb[:0:-1]], axis=0)
    return jnp.transpose(taps, (1, 0, 2))


def _hyena(p_hy, conv_w, conv_b, filt, bias):
    B, L, _ = p_hy.shape
    C = HY_WIDTH
    taps = _hyena_filters(L, *filt)
    v, x1, x2 = _hy_dwconv(p_hy, conv_w, conv_b)
    gates = (x1, x2)
    u = v
    if 2 * L <= 1024:
        cst = _dense_consts(L)
        spec = _dense_spectrum(taps, cst["fd_full"])
        for o in range(HY_ORDER):
            u = _dense_conv(u, cst["fd_half"], cst["fi"], spec[o], bias[o], gates[o])
        return u
    cst = _fft_consts(L)
    Na, Nb, nkp = cst["Na"], cst["Nb"], cst["nkp"]
    W = Nb * C
    ta = _fft1(taps.reshape(HY_ORDER, Na, W), cst["f1_full"])
    spec = _fft2_spectrum(ta.reshape(HY_ORDER, 2, nkp, Nb, C), cst["fk"])
    for o in range(HY_ORDER):
        u2 = u.reshape(B, Na // 2, W)
        a = _fft1(u2, cst["f1_half"])
        bm = _fft2_conv(a.reshape(B, 2, nkp, Nb, C), cst["fk"], cst["gk"], spec[o])
        bias_t = jnp.tile(bias[o], Nb).reshape(1, W)
        u = _fft3(bm.reshape(B, 2 * nkp, W), cst["f3"], u2, bias_t,
                  gates[o].reshape(B, Na // 2, W)).reshape(B, L, C)
    return u


def _gelu_tanh(x):
    return 0.5 * x * (1.0 + jnp.tanh(math.sqrt(2.0 / math.pi) * (x + 0.044715 * (x * x * x))))


def _merge_kernel(x_ref, g1_ref, pg_ref, yhy_ref, ys5_ref, ps5_ref, d_ref, ymla_ref,
                  wglu_ref, whyr, ws5r, wmlar, wo_ref, o_ref):
    D = x_ref.shape[-1]
    ys = _gelu_tanh(d_ref[...] * ps5_ref[...].astype(F32) + ys5_ref[...])
    ag = _dot(ys.astype(BF16), wglu_ref[...])
    y_s5 = ag[:, :S5_WIDTH] * _sigmoid(ag[:, S5_WIDTH:])
    pg = pg_ref[...].astype(F32)
    merged = (_sigmoid(pg[:, :D]) * _dot(yhy_ref[...].astype(BF16), whyr[...])
              + _sigmoid(pg[:, D:2 * D]) * _dot(y_s5.astype(BF16), ws5r[...])
              + _sigmoid(pg[:, 2 * D:]) * _dot(ymla_ref[...], wmlar[...]))
    o_ref[...] = x_ref[...] + g1_ref[...] * _dot(merged.astype(BF16), wo_ref[...])


def _merge(x, g1, pg, y_hy, ys5, ps5, d_skip, y_mla, wts, tm):
    B, L, D = x.shape
    tok = lambda n: pl.BlockSpec((None, tm, n), lambda b, i: (b, i, 0))
    per_b = pl.BlockSpec((None, 1, D), lambda b, i: (b, 0, 0))
    ws = [wts[n] for n in ("w_glu", "w_br_hy", "w_br_s5", "w_br_mla", "w_o")]
    return pl.pallas_call(
        _merge_kernel,
        grid=(B, L // tm),
        in_specs=[tok(D), per_b, tok(3 * D), tok(HY_WIDTH), tok(S5_WIDTH), tok(S5_WIDTH),
                  _const_spec((1, S5_WIDTH)), tok(MLA_HEADS * MLA_V)] + [_const_spec(w.shape) for w in ws],
        out_specs=tok(D),
        out_shape=jax.ShapeDtypeStruct((B, L, D), F32),
        compiler_params=_cparams(("parallel", "parallel")),
    )(x, g1, pg, y_hy, ys5, ps5, d_skip.reshape(1, -1), y_mla, *ws)


def _ffn_up_kernel(x_ref, ge_ref, sh_ref, w_ref, o_ref):
    hb = (_rms(x_ref[...], ge_ref[...]) + sh_ref[...]).astype(BF16)
    o_ref[...] = _dot(hb, w_ref[...]).astype(o_ref.dtype)


def _ffn_up(x, ge, sh, w_up, tm):
    B, L, D = x.shape
    n = w_up.shape[1]
    per_b = pl.BlockSpec((None, 1, D), lambda b, i: (b, 0, 0))
    return pl.pallas_call(
        _ffn_up_kernel,
        grid=(B, L // tm),
        in_specs=[pl.BlockSpec((None, tm, D), lambda b, i: (b, i, 0)), per_b, per_b, _const_spec(w_up.shape)],
        out_specs=pl.BlockSpec((None, tm, n), lambda b, i: (b, i, 0)),
        out_shape=jax.ShapeDtypeStruct((B, L, n), BF16),
        compiler_params=_cparams(("parallel", "parallel")),
    )(x, ge, sh, w_up)


def _ffn_down_kernel(final, nt, a_ref, ap_ref, an_ref, s2_ref, cw_ref, cb_ref, wd_ref, x_ref, g2_ref, *rest):
    i = pl.program_id(1)
    ab = a_ref[...]
    tm = ab.shape[0]
    hid = ab.shape[1] // 2
    sh = _dot(s2_ref[...], ab)
    row8 = lax.broadcasted_iota(jnp.int32, (8, 1), 0)
    prev_row = jnp.where(i > 0, ap_ref[7:8, :].astype(F32), 0.0)
    next_row = jnp.where(i < nt - 1, an_ref[0:1, :].astype(F32), 0.0)
    am = jnp.concatenate([sh[0:8] + jnp.where(row8 == 0, prev_row, 0.0), sh[8:tm]], axis=0)
    ap = jnp.concatenate([sh[tm:2 * tm - 8], sh[2 * tm - 8:] + jnp.where(row8 == 7, next_row, 0.0)], axis=0)
    cw = cw_ref[...]
    c = am * cw[0:1] + ab.astype(F32) * cw[1:2] + ap * cw[2:3] + cb_ref[...]
    act = (_silu(c[:, :hid]) * c[:, hid:]).astype(BF16)
    y = x_ref[...] + g2_ref[...] * _dot(act, wd_ref[...])
    if final:
        fg_ref, o_ref = rest
        o_ref[...] = _rms(y, fg_ref[...])
    else:
        rest[0][...] = y


def _ffn_down(a, conv_w, conv_b, w_down, x, g2, final_g, tm):
    B, L, n = a.shape
    D = x.shape[-1]
    nt = L // tm
    r8 = tm // 8
    final = final_g is not None
    shift2 = jnp.concatenate([jnp.eye(tm, k=-1, dtype=BF16), jnp.eye(tm, k=1, dtype=BF16)], axis=0)
    in_specs = [pl.BlockSpec((None, tm, n), lambda b, i: (b, i, 0)),
                pl.BlockSpec((None, 8, n), lambda b, i: (b, jnp.maximum(i * r8 - 1, 0), 0)),
                pl.BlockSpec((None, 8, n), lambda b, i: (b, jnp.minimum((i + 1) * r8, L // 8 - 1), 0)),
                _const_spec((2 * tm, tm)), _const_spec((3, n)), _const_spec((1, n)), _const_spec(w_down.shape),
                pl.BlockSpec((None, tm, D), lambda b, i: (b, i, 0)),
                pl.BlockSpec((None, 1, D), lambda b, i: (b, 0, 0))]
    args = [a, a, a, shift2, conv_w, conv_b.reshape(1, n), w_down, x, g2]
    if final:
        in_specs.append(_const_spec((1, D)))
        args.append(final_g.reshape(1, D))
    return pl.pallas_call(
        functools.partial(_ffn_down_kernel, final, nt),
        grid=(B, nt),
        in_specs=in_specs,
        out_specs=pl.BlockSpec((None, tm, D), lambda b, i: (b, i, 0)),
        out_shape=jax.ShapeDtypeStruct((B, L, D), F32),
        compiler_params=_cparams(("parallel", "parallel")),
    )(*args)


def _rope_tables(L, rope):
    half = MLA_ROPE // 2
    cos = jnp.ones((L, HEAD_PAD), F32)
    sa = jnp.zeros((L, HEAD_PAD), F32)
    sb = jnp.zeros((L, HEAD_PAD), F32)
    if not rope:
        return cos, sa, sb
    rows = L // GRID_W
    row = jnp.repeat(jnp.arange(rows, dtype=F32), GRID_W)
    col = jnp.tile(jnp.arange(GRID_W, dtype=F32), rows)
    n_ax = MLA_ROPE // 4
    inv = ROPE_THETA ** (-jnp.arange(n_ax, dtype=F32) / n_ax)
    ang = jnp.concatenate([row[:, None] * inv, col[:, None] * inv], axis=-1)
    c, s = jnp.cos(ang), jnp.sin(ang)
    cos = cos.at[:, MLA_NOPE:MLA_NOPE + half].set(c).at[:, MLA_NOPE + half:MLA_NOPE + 2 * half].set(c)
    sa = sa.at[:, MLA_NOPE:MLA_NOPE + half].set(-s)
    sb = sb.at[:, MLA_NOPE + half:MLA_NOPE + 2 * half].set(s)
    return cos, sa, sb


def _layer_weights(w_in, g_q, w_uq, g_kv, w_ukv, s5_w_glu, w_br_hy, w_br_s5, w_br_mla, w_o):
    D = w_in.shape[0]
    H = MLA_HEADS
    c_kv = S5_WIDTH
    c_kr = c_kv + w_ukv.shape[0]
    c_q = c_kr + MLA_ROPE
    c_hy = c_q + w_uq.shape[0]
    c_gate = c_hy + (HY_ORDER + 1) * HY_WIDTH
    bf = lambda a: a.astype(BF16)
    w_kr = jnp.zeros((D, HEAD_PAD), F32).at[:, MLA_NOPE:MLA_NOPE + MLA_ROPE].set(w_in[:, c_kr:c_q])
    ukv = w_ukv.reshape(-1, H, MLA_NOPE + MLA_V)
    w_uk = jnp.zeros((ukv.shape[0], H, HEAD_PAD), F32).at[:, :, :MLA_NOPE].set(ukv[:, :, :MLA_NOPE])
    w_uv = jnp.zeros((ukv.shape[0], H, HEAD_PAD), F32).at[:, :, :MLA_V].set(ukv[:, :, MLA_NOPE:])
    uq = w_uq.reshape(-1, H, MLA_NOPE + MLA_ROPE)
    w_uqp = jnp.zeros((uq.shape[0], H, HEAD_PAD), F32).at[:, :, :MLA_NOPE + MLA_ROPE].set(uq)
    return dict(
        w_s5=bf(w_in[:, :c_kv]), w_kv=bf(w_in[:, c_kv:c_kr]), w_kr=bf(w_kr), w_q=bf(w_in[:, c_q:c_hy]),
        w_hy=bf(w_in[:, c_hy:c_gate]), w_gate=bf(w_in[:, c_gate:]),
        g_kv=g_kv.reshape(1, -1), g_q=g_q.reshape(1, -1),
        w_uk=bf(w_uk.reshape(-1, H * HEAD_PAD)), w_uv=bf(w_uv.reshape(-1, H * HEAD_PAD)),
        w_uq=bf(w_uqp.reshape(-1, H * HEAD_PAD)),
        w_glu=bf(s5_w_glu), w_br_hy=bf(w_br_hy), w_br_s5=bf(w_br_s5), w_br_mla=bf(w_br_mla), w_o=bf(w_o))


def _mixer(x, mods, n1g, wts, s5_mats, s5_h0, s5_d, hy_args, rope_tabs, ctx_kv, full, tm, tq, tk):
    sh1, sc1, g1 = mods
    ge = n1g * (1.0 + sc1)
    outs = _in_proj(x, ge, sh1, rope_tabs, wts, full, tm)
    p_s5, k, v = outs[:3]
    y_s5, hfin = _s5_scan(p_s5, s5_mats, s5_h0)
    if not full:
        return None, hfin, (k, v)
    q, p_hy, p_gate = outs[3:]
    kvs = [(k, v)] + ([ctx_kv] if ctx_kv is not None else [])
    tks = [tk] + ([ctx_kv[0].shape[2]] if ctx_kv is not None else [])
    y_mla = _attention(q, kvs, tq, tks)
    y_hy = _hyena(p_hy, *hy_args)
    x_new = _merge(x, g1, p_gate, y_hy, y_s5, p_s5, s5_d, y_mla, wts, tm)
    return x_new, hfin, (k, v)


def _ffn(x, mods, n2g, w_up, conv_w, conv_b, w_down, final_g, tm):
    sh2, sc2, g2 = mods
    a = _ffn_up(x, n2g * (1.0 + sc2), sh2, w_up, tm)
    return _ffn_down(a, conv_w, conv_b, w_down, x, g2, final_g, tm)


def kernel(x, c, ctx, c_ctx, w_mod, b_mod, norm1_g, norm2_g, w_in, hy_conv_w, hy_conv_b, hy_f_w1, hy_f_b1, hy_f_w2, hy_f_b2, hy_f_w3, hy_f_freq, hy_f_decay, hy_bias, s5_lam_re, s5_lam_im, s5_log_step, s5_b_re, s5_b_im, s5_c_re, s5_c_im, s5_d, s5_w_glu, mla_g_q, mla_w_uq, mla_g_kv, mla_w_ukv, w_br_hy, w_br_s5, w_br_mla, w_o, ffn_w_up, ffn_conv_w, ffn_conv_b, ffn_w_down, final_g):
    B, L, D = x.shape
    Lc = ctx.shape[1]
    depth = w_mod.shape[0]
    tm = min(256, L)
    tmc = min(256, Lc)
    tq = min(512, L)
    tk = min(1024, L)
    rope_lat = _rope_tables(L, True)
    rope_ctx = _rope_tables(Lc, False)
    mrows = 8 * ((B + 1 + 7) // 8)
    c_all = jnp.zeros((mrows, D), F32).at[:B].set(c).at[B].set(c_ctx)
    zeros_h = (jnp.zeros((S5_GROUPS, B, 2 * S5_STATE), F32),) * 2
    xc = ctx
    for i in range(depth):
        ctx_out = i < depth - 1
        mod = _modulation(c_all, w_mod[i], b_mod[i])
        m_lat = [m[:, None, :] for m in jnp.split(mod[:B], 6, axis=-1)]
        m_ctx = [jnp.broadcast_to(m[None, None, :], (B, 1, D)) for m in jnp.split(mod[B], 6, axis=-1)]
        wts = _layer_weights(w_in[i], mla_g_q[i], mla_w_uq[i], mla_g_kv[i], mla_w_ukv[i], s5_w_glu[i],
                             w_br_hy[i], w_br_s5[i], w_br_mla[i], w_o[i])
        s5_mats = _s5_matrices(s5_lam_re[i], s5_lam_im[i], s5_log_step[i], s5_b_re[i], s5_b_im[i],
                               s5_c_re[i], s5_c_im[i])
        filt = (hy_f_w1[i], hy_f_b1[i], hy_f_w2[i], hy_f_b2[i], hy_f_w3[i], hy_f_freq[i], hy_f_decay[i])
        hy_args = (hy_conv_w[i], hy_conv_b[i], filt, hy_bias[i])
        n1g, n2g = norm1_g[i][None, None, :], norm2_g[i][None, None, :]
        w_up, w_down = ffn_w_up[i].astype(BF16), ffn_w_down[i].astype(BF16)

        xc_new, hc_fin, ctx_kv = _mixer(xc, m_ctx[0:3], n1g, wts, s5_mats, zeros_h, s5_d[i], hy_args,
                                        rope_ctx, None, ctx_out, tmc, tmc, tmc)
        x, _, _ = _mixer(x, m_lat[0:3], n1g, wts, s5_mats, hc_fin, s5_d[i], hy_args,
                         rope_lat, ctx_kv, True, tm, tq, tk)
        last = i == depth - 1
        x = _ffn(x, m_lat[3:6], n2g, w_up, ffn_conv_w[i], ffn_conv_b[i], w_down,
                 final_g if last else None, tm)
        if ctx_out:
            xc = _ffn(xc_new, m_ctx[3:6], n2g, w_up, ffn_conv_w[i], ffn_conv_b[i], w_down, None, tmc)
    return x
```

```python
import functools
import math

import jax
import jax.numpy as jnp
from jax import lax
from jax.experimental import pallas as pl
from jax.experimental.pallas import tpu as pltpu

F32 = jnp.float32
BF16 = jnp.bfloat16

EPS = 1e-6
GRID_W = 64
HY_WIDTH = 384
HY_ORDER = 2
HY_BANDS = 8
S5_WIDTH = 384
S5_GROUP = 16
S5_GROUPS = S5_WIDTH // S5_GROUP
S5_STATE = 64
S5_CHUNK = 128
MLA_HEADS = 8
MLA_NOPE = 64
MLA_ROPE = 32
MLA_V = 64
MLA_SCALE = (MLA_NOPE + MLA_ROPE) ** -0.5
LOG2E = math.log2(math.e)
ROPE_THETA = 10000.0
HEAD_PAD = 128
FFT_NB = 128
VMEM_LIMIT = 56 * 1024 * 1024


def _cparams(sem):
    return pltpu.CompilerParams(dimension_semantics=sem, vmem_limit_bytes=VMEM_LIMIT)


def _split(x):
    hi = x.astype(BF16)
    lo = (x - hi.astype(F32)).astype(BF16)
    return hi, lo


def _dot(a, b):
    return jnp.dot(a, b, preferred_element_type=F32)


def _dot3(ah, al, bh, bl):
    return _dot(ah, bh) + _dot(al, bh) + _dot(ah, bl)


def _sigmoid(x):
    return 1.0 / (1.0 + jnp.exp(-x))


def _silu(x):
    return x * _sigmoid(x)


def _rms(x, g):
    return x * lax.rsqrt(jnp.mean(x * x, axis=-1, keepdims=True) + EPS) * g


def _const_spec(shape):
    nd = len(shape)
    return pl.BlockSpec(shape, lambda *_: (0,) * nd)


def _mod_kernel(c_ref, w_ref, b_ref, o_ref):
    ch, cl = _split(_silu(c_ref[...]))
    wh, wl = _split(w_ref[...])
    o_ref[...] = _dot3(ch, cl, wh, wl) + b_ref[...]


def _modulation(c_all, w, b):
    m, d = c_all.shape
    n = w.shape[1]
    tn = 1024
    return pl.pallas_call(
        _mod_kernel,
        grid=(n // tn,),
        in_specs=[pl.BlockSpec((m, d), lambda j: (0, 0)),
                  pl.BlockSpec((d, tn), lambda j: (0, j)),
                  pl.BlockSpec((1, tn), lambda j: (0, j))],
        out_specs=pl.BlockSpec((m, tn), lambda j: (0, j)),
        out_shape=jax.ShapeDtypeStruct((m, n), F32),
        compiler_params=_cparams(("arbitrary",)),
    )(c_all, w, b.reshape(1, n))


def _rope(x, cos, sa, sb):
    return (x * cos + pltpu.roll(x, HEAD_PAD - MLA_ROPE // 2, 1) * sa
            + pltpu.roll(x, MLA_ROPE // 2, 1) * sb)


def _in_kernel(full, x_ref, ge_ref, sh_ref, cos_ref, sa_ref, sb_ref,
               w_s5, w_kv, w_kr, gkv_ref, w_uk, w_uv, *rest):
    if full:
        (w_q, gq_ref, w_uq, w_hy, w_gate,
         o_s5, o_k, o_v, o_q, o_hy, o_gate) = rest
    else:
        o_s5, o_k, o_v = rest
    hb = (_rms(x_ref[...], ge_ref[...]) + sh_ref[...]).astype(BF16)
    cos, sa, sb = cos_ref[...], sa_ref[...], sb_ref[...]
    o_s5[...] = _dot(hb, w_s5[...]).astype(o_s5.dtype)
    nkv = _rms(_dot(hb, w_kv[...]), gkv_ref[...]).astype(BF16)
    vfull = _dot(nkv, w_uv[...])
    knope = _dot(nkv, w_uk[...])
    kr = _rope(_dot(hb, w_kr[...]), cos, sa, sb)
    ones_col = (lax.broadcasted_iota(jnp.int32, (1, HEAD_PAD), 1) == MLA_V).astype(F32)
    for h in range(MLA_HEADS):
        o_k[h] = (knope[:, h * HEAD_PAD:(h + 1) * HEAD_PAD] + kr).astype(o_k.dtype)
        o_v[h] = (vfull[:, h * HEAD_PAD:(h + 1) * HEAD_PAD] + ones_col).astype(o_v.dtype)
    if full:
        nq = _rms(_dot(hb, w_q[...]), gq_ref[...]).astype(BF16)
        qf = _dot(nq, w_uq[...])
        for h in range(MLA_HEADS):
            qh = _rope(qf[:, h * HEAD_PAD:(h + 1) * HEAD_PAD], cos, sa, sb) * (MLA_SCALE * LOG2E)
            o_q[h] = qh.astype(o_q.dtype)
        o_hy[...] = _dot(hb, w_hy[...]).astype(o_hy.dtype)
        o_gate[...] = _dot(hb, w_gate[...]).astype(o_gate.dtype)


def _in_proj(x, ge, sh, rope_tabs, wts, full, tm):
    B, L, D = x.shape
    H = MLA_HEADS
    tok = lambda n: pl.BlockSpec((None, tm, n), lambda b, i: (b, i, 0))
    per_b = pl.BlockSpec((None, 1, D), lambda b, i: (b, 0, 0))
    tab = pl.BlockSpec((tm, HEAD_PAD), lambda b, i: (i, 0))
    head = pl.BlockSpec((None, H, tm, HEAD_PAD), lambda b, i: (b, 0, i, 0))
    names = ["w_s5", "w_kv", "w_kr", "g_kv", "w_uk", "w_uv"]
    if full:
        names += ["w_q", "g_q", "w_uq", "w_hy", "w_gate"]
    ws = [wts[n] for n in names]
    in_specs = [tok(D), per_b, per_b, tab, tab, tab] + [_const_spec(w.shape) for w in ws]
    out_specs = [tok(S5_WIDTH), head, head]
    out_shape = [jax.ShapeDtypeStruct((B, L, S5_WIDTH), BF16),
                 jax.ShapeDtypeStruct((B, H, L, HEAD_PAD), BF16),
                 jax.ShapeDtypeStruct((B, H, L, HEAD_PAD), BF16)]
    if full:
        n_hy = (HY_ORDER + 1) * HY_WIDTH
        out_specs += [head, tok(n_hy), tok(3 * D)]
        out_shape += [jax.ShapeDtypeStruct((B, H, L, HEAD_PAD), BF16),
                      jax.ShapeDtypeStruct((B, L, n_hy), BF16),
                      jax.ShapeDtypeStruct((B, L, 3 * D), BF16)]
    return pl.pallas_call(
        functools.partial(_in_kernel, full),
        grid=(B, L // tm),
        in_specs=in_specs, out_specs=out_specs, out_shape=out_shape,
        compiler_params=_cparams(("parallel", "parallel")),
    )(x, ge, sh, *rope_tabs, *ws)


def _attn_kernel(segs, tq, q_ref, *refs):
    o_ref = refs[-1]
    nh = 2
    nsp = 1
    tr = tq // nsp
    chains = [(hh, sp) for hh in range(nh) for sp in range(nsp)]
    qs = [q_ref[hh, sp * tr:(sp + 1) * tr, :] for hh, sp in chains]
    carry = tuple((jnp.full((tr, 1), -jnp.inf, F32), jnp.zeros((tr, HEAD_PAD), F32)) for _ in chains)
    for si, (lk, tk) in enumerate(segs):
        k_ref, v_ref = refs[2 * si], refs[2 * si + 1]

        def body(j, c, k_ref=k_ref, v_ref=v_ref, tk=tk):
            off = pl.multiple_of(j * tk, tk)
            out = []
            for ci, (hh, _) in enumerate(chains):
                m, acc = c[ci]
                kb = k_ref[hh, pl.ds(off, tk), :]
                s = lax.dot_general(qs[ci], kb, (((1,), (1,)), ((), ())), preferred_element_type=F32)
                m_new = jnp.maximum(m, jnp.max(s, axis=-1, keepdims=True))
                p = jnp.exp2(s - m_new).astype(BF16)
                acc = jnp.exp2(m - m_new) * acc + _dot(p, v_ref[hh, pl.ds(off, tk), :])
                out.append((m_new, acc))
            return tuple(out)

        carry = lax.fori_loop(0, lk // tk, body, carry)
    outs = [acc[:, :MLA_V] / acc[:, MLA_V:MLA_V + 1] for _, acc in carry]
    heads = [jnp.concatenate(outs[hh * nsp:(hh + 1) * nsp], axis=0) for hh in range(nh)]
    o_ref[...] = jnp.concatenate(heads, axis=1).astype(o_ref.dtype)


def _attention(q, kvs, tq, tks):
    B, H, Lq, _ = q.shape
    segs = tuple((k.shape[2], tk) for (k, _), tk in zip(kvs, tks))
    in_specs = [pl.BlockSpec((None, 2, tq, HEAD_PAD), lambda b, j, i: (b, j, i, 0))]
    args = [q]
    for k, v in kvs:
        lk = k.shape[2]
        in_specs += [pl.BlockSpec((None, 2, lk, HEAD_PAD), lambda b, j, i: (b, j, 0, 0))] * 2
        args += [k, v]
    return pl.pallas_call(
        functools.partial(_attn_kernel, segs, tq),
        grid=(B, H // 2, Lq // tq),
        in_specs=in_specs,
        out_specs=pl.BlockSpec((None, tq, 2 * MLA_V), lambda b, j, i: (b, i, j)),
        out_shape=jax.ShapeDtypeStruct((B, Lq, H * MLA_V), BF16),
        compiler_params=_cparams(("parallel", "parallel", "arbitrary")),
    )(*args)


def _s5_pack_kernel(cp, p_ref, o_ref):
    x = p_ref[...].astype(F32)
    for kk in range(cp):
        o_ref[kk] = x[kk * S5_CHUNK:(kk + 1) * S5_CHUNK, :].T


def _s5_pack(p):
    B, L, C = p.shape
    T = S5_CHUNK
    nch = L // T
    cp = min(8, nch)
    return pl.pallas_call(
        functools.partial(_s5_pack_kernel, cp),
        grid=(B, nch // cp),
        in_specs=[pl.BlockSpec((None, cp * T, C), lambda b, i: (b, i, 0))],
        out_specs=pl.BlockSpec((None, cp, C, T), lambda b, i: (b, i, 0, 0)),
        out_shape=jax.ShapeDtypeStruct((B, nch, C, T), F32),
        compiler_params=_cparams(("parallel", "parallel")),
    )(p)


def _s5_unpack_kernel(cp, z_ref, o_ref):
    for kk in range(cp):
        o_ref[kk * S5_CHUNK:(kk + 1) * S5_CHUNK, :] = z_ref[kk].T


def _s5_unpack(z):
    B, nch, C, T = z.shape
    cp = min(8, nch)
    return pl.pallas_call(
        functools.partial(_s5_unpack_kernel, cp),
        grid=(B, nch // cp),
        in_specs=[pl.BlockSpec((None, cp, C, T), lambda b, i: (b, i, 0, 0))],
        out_specs=pl.BlockSpec((None, cp * T, C), lambda b, i: (b, i, 0)),
        out_shape=jax.ShapeDtypeStruct((B, nch * T, C), F32),
        compiler_params=_cparams(("parallel", "parallel")),
    )(z)


def _toeplitz_kernel(w_ref, o_ref):
    T, cg = S5_CHUNK, S5_GROUP

    def body(cp, carry):
        r0 = pl.multiple_of(cp * T, T)
        for c in range(cg):
            w = w_ref[pl.ds(cp * cg + c, 1), :]
            blk = pltpu.roll(jnp.broadcast_to(w, (T, 2 * T)), 0, 1, stride=1, stride_axis=0)
            o_ref[pl.ds(r0, T), c * T:(c + 1) * T] = blk[:, :T].astype(BF16)
        return carry

    lax.fori_loop(0, cg, body, 0)


def _toeplitz(w):
    G, npair, W = w.shape
    n = S5_GROUP * S5_CHUNK
    return pl.pallas_call(
        _toeplitz_kernel,
        grid=(G,),
        in_specs=[pl.BlockSpec((None, npair, W), lambda g: (g, 0, 0))],
        out_specs=pl.BlockSpec((None, n, n), lambda g: (g, 0, 0)),
        out_shape=jax.ShapeDtypeStruct((G, n, n), BF16),
        compiler_params=_cparams(("parallel",)),
    )(w)


def _s5_kernel(nch, nb, u_ref, mi_ref, min_ref, mout_ref, atr_ref, ati_ref, h0r_ref, h0i_ref,
               y_ref, hfr_ref, hfi_ref, sr_ref, si_ref, hr_ref, hi_ref):
    ns, cg, T = S5_STATE, S5_GROUP, S5_CHUNK
    u = jnp.concatenate([u_ref[:, c, :] for c in range(cg)], axis=1).astype(BF16)
    s = _dot(u, min_ref[...])
    sr_ref[...] = s[:, :2 * ns]
    si_ref[...] = s[:, 2 * ns:]
    atr, ati = atr_ref[...], ati_ref[...]
    fwd = lax.broadcasted_iota(jnp.int32, (1, 2 * ns), 1) < ns

    def body(i, carry):
        out = []
        for b in range(nb):
            hr, hi = carry[2 * b], carry[2 * b + 1]
            rf = b * nch + i
            rb = b * nch + (nch - 1 - i)
            hr_ref[pl.ds(rf, 1), 0:ns] = hr[:, 0:ns]
            hi_ref[pl.ds(rf, 1), 0:ns] = hi[:, 0:ns]
            hr_ref[pl.ds(rb, 1), ns:2 * ns] = hr[:, ns:2 * ns]
            hi_ref[pl.ds(rb, 1), ns:2 * ns] = hi[:, ns:2 * ns]
            s_r = jnp.where(fwd, sr_ref[pl.ds(rf, 1), :], sr_ref[pl.ds(rb, 1), :])
            s_i = jnp.where(fwd, si_ref[pl.ds(rf, 1), :], si_ref[pl.ds(rb, 1), :])
            out += [atr * hr - ati * hi + s_r, atr * hi + ati * hr + s_i]
        return tuple(out)

    init = []
    for b in range(nb):
        init += [h0r_ref[b:b + 1, :], h0i_ref[b:b + 1, :]]
    fin = lax.fori_loop(0, nch, body, tuple(init))
    for b in range(nb):
        hfr_ref[b:b + 1, :] = fin[2 * b]
        hfi_ref[b:b + 1, :] = fin[2 * b + 1]
    hcat = jnp.concatenate([hr_ref[...], hi_ref[...]], axis=1).astype(BF16)
    cb = 4
    for j in range(cg // cb):
        cols = slice(j * cb * T, (j + 1) * cb * T)
        y = _dot(u, mi_ref[:, cols]) + _dot(hcat, mout_ref[:, cols])
        for c in range(cb):
            y_ref[:, j * cb + c, :] = y[:, c * T:(c + 1) * T]


def _cmul(ar, ai, br, bi):
    return ar * br - ai * bi, ar * bi + ai * br


def _s5_matrices(lam_re, lam_im, log_step, b_re, b_im, c_re, c_im):
    hp = lax.Precision.HIGHEST
    T = S5_CHUNK
    step = jnp.exp(log_step)[..., None]
    def apow(tau):
        mag = jnp.exp(lam_re[:, :, None, :] * step[:, :, None, :] * tau[None, None, :, None])
        ang = lam_im[:, :, None, :] * step[:, :, None, :] * tau[None, None, :, None]
        return mag * jnp.cos(ang), mag * jnp.sin(ang)
    a_r, a_i = apow(jnp.ones((1,), F32))
    a_r, a_i = a_r[:, :, 0], a_i[:, :, 0]
    den = lam_re ** 2 + lam_im ** 2
    q_r, q_i = _cmul(a_r - 1.0, a_i, lam_re / den, -lam_im / den)
    bb_r, bb_i = _cmul(q_r[..., None], q_i[..., None], b_re, b_im)
    tt = jnp.arange(T, dtype=F32)
    p_r, p_i = apow(tt)
    cb_r, cb_i = _cmul(jnp.swapaxes(c_re, -1, -2)[..., :, :, None], jnp.swapaxes(c_im, -1, -2)[..., :, :, None],
                       bb_r[..., :, None, :], bb_i[..., :, None, :])
    kk = (jnp.einsum('dgtn,dgncx->dgtcx', p_r, cb_r, precision=hp)
          - jnp.einsum('dgtn,dgncx->dgtcx', p_i, cb_i, precision=hp))
    kf, kb = kk[0], kk[1]
    G, cg = kf.shape[0], kf.shape[-1]
    wlag = jnp.concatenate([kf[:, :1] + kb[:, :1], kf[:, 1:], jnp.zeros_like(kf[:, :1]), kb[:, :0:-1]], axis=1)
    wlag = jnp.transpose(wlag, (0, 3, 2, 1)).reshape(G, cg * cg, 2 * T)
    m_intra = _toeplitz(wlag)
    def instate(pr, pi, br, bi):
        return _cmul(pr[:, None, :, :], pi[:, None, :, :],
                     jnp.swapaxes(br, -1, -2)[:, :, None, :], jnp.swapaxes(bi, -1, -2)[:, :, None, :])
    f_r, f_i = instate(p_r[0][:, ::-1], p_i[0][:, ::-1], bb_r[0], bb_i[0])
    g_r, g_i = instate(p_r[1], p_i[1], bb_r[1], bb_i[1])
    m_in = jnp.concatenate([f_r, g_r, f_i, g_i], axis=-1).reshape(G, cg * T, 4 * S5_STATE)
    p1_r, p1_i = apow(tt + 1.0)
    def outstate(pr, pi, cr, ci):
        xr, xi = _cmul(jnp.swapaxes(pr, 1, 2)[:, :, None, :], jnp.swapaxes(pi, 1, 2)[:, :, None, :],
                       jnp.swapaxes(cr, 1, 2)[:, :, :, None], jnp.swapaxes(ci, 1, 2)[:, :, :, None])
        return xr, -xi
    of_r, of_i = outstate(p1_r[0], p1_i[0], c_re[0], c_im[0])
    ob_r, ob_i = outstate(p1_r[1][:, ::-1], p1_i[1][:, ::-1], c_re[1], c_im[1])
    m_out = jnp.concatenate([of_r, ob_r, of_i, ob_i], axis=1).reshape(G, 4 * S5_STATE, cg * T)
    at_r, at_i = apow(jnp.full((1,), float(T), F32))
    at_r = jnp.concatenate([at_r[0, :, 0], at_r[1, :, 0]], axis=-1)[:, None, :]
    at_i = jnp.concatenate([at_i[0, :, 0], at_i[1, :, 0]], axis=-1)[:, None, :]
    return m_intra, m_in.astype(BF16), m_out.astype(BF16), at_r, at_i


def _s5_scan(u, mats, h0):
    B, L, C = u.shape
    T, G, cg, ns = S5_CHUNK, S5_GROUPS, S5_GROUP, S5_STATE
    nch = L // T
    R = nch * B
    m_intra, m_in, m_out, at_r, at_i = mats
    z = _s5_pack(u).reshape(R, C, T)
    gspec = lambda r, c: pl.BlockSpec((None, r, c), lambda g: (g, 0, 0))
    zspec = pl.BlockSpec((R, cg, T), lambda g: (0, g, 0))
    y, hfr, hfi = pl.pallas_call(
        functools.partial(_s5_kernel, nch, B),
        grid=(G,),
        in_specs=[zspec, gspec(T * cg, T * cg), gspec(T * cg, 4 * ns), gspec(4 * ns, T * cg),
                  gspec(1, 2 * ns), gspec(1, 2 * ns), gspec(B, 2 * ns), gspec(B, 2 * ns)],
        out_specs=[zspec, gspec(B, 2 * ns), gspec(B, 2 * ns)],
        out_shape=[jax.ShapeDtypeStruct((R, C, T), F32),
                   jax.ShapeDtypeStruct((G, B, 2 * ns), F32),
                   jax.ShapeDtypeStruct((G, B, 2 * ns), F32)],
        scratch_shapes=[pltpu.VMEM((R, 2 * ns), F32)] * 4,
        compiler_params=_cparams(("parallel",)),
    )(z, m_intra, m_in, m_out, at_r, at_i, h0[0], h0[1])
    return _s5_unpack(y.reshape(B, nch, C, T)), (hfr, hfi)


def _shift_rows(x, prev_row, next_row):
    n = x.shape[0]
    row = lax.broadcasted_iota(jnp.int32, x.shape, 0)
    xm = jnp.where(row == 0, prev_row, pltpu.roll(x, 1, 0))
    xp = jnp.where(row == n - 1, next_row, pltpu.roll(x, n - 1, 0))
    return xm, xp


def _dwconv_kernel(p_ref, w_ref, b_ref, o_ref):
    x = p_ref[...].astype(F32)
    w = w_ref[...]
    xm, xp = _shift_rows(x, 0.0, 0.0)
    o_ref[...] = xm * w[0:1] + x * w[1:2] + xp * w[2:3] + b_ref[...]


def _hy_dwconv(p, w, b):
    B, L, n = p.shape
    tc = 128
    nblk = HY_WIDTH // tc
    outs = []
    for part in range(n // HY_WIDTH):
        outs.append(pl.pallas_call(
            _dwconv_kernel,
            grid=(B, nblk),
            in_specs=[pl.BlockSpec((None, L, tc), lambda bb, j, part=part: (bb, 0, part * nblk + j)),
                      pl.BlockSpec((3, tc), lambda bb, j, part=part: (0, part * nblk + j)),
                      pl.BlockSpec((1, tc), lambda bb, j, part=part: (0, part * nblk + j))],
            out_specs=pl.BlockSpec((None, L, tc), lambda bb, j: (bb, 0, j)),
            out_shape=jax.ShapeDtypeStruct((B, L, HY_WIDTH), F32),
            compiler_params=_cparams(("parallel", "parallel")),
        )(p, w, b.reshape(1, n)))
    return outs


def _fft1_kernel(fh_ref, fl_ref, u_ref, o_ref):
    uh, ul = _split(u_ref[...])
    o_ref[...] = _dot3(fh_ref[...], fl_ref[...], uh, ul)


def _fft1(u2, f1):
    B, K, W = u2.shape
    M = f1[0].shape[0]
    tn = min(W, 6144)
    return pl.pallas_call(
        _fft1_kernel,
        grid=(B, W // tn),
        in_specs=[_const_spec((M, K)), _const_spec((M, K)),
                  pl.BlockSpec((None, K, tn), lambda b, j: (b, 0, j))],
        out_specs=pl.BlockSpec((None, M, tn), lambda b, j: (b, 0, j)),
        out_shape=jax.ShapeDtypeStruct((B, M, W), F32),
        compiler_params=_cparams(("parallel", "parallel")),
    )(f1[0], f1[1], u2)


def _fft2_kernel(conv, nbat, fh_ref, fl_ref, *rest):
    nb = FFT_NB
    if conv:
        gh_ref, gl_ref, h_ref, a_ref, o_ref = rest
        hsp = h_ref[...]
        hr, hi = hsp[:nb], hsp[nb:]
    else:
        a_ref, o_ref = rest
    for b in range(nbat):
        a = jnp.concatenate([a_ref[b, 0], a_ref[b, 1]], axis=0)
        ah, al = _split(a)
        x = _dot3(fh_ref[...], fl_ref[...], ah, al)
        if not conv:
            o_ref[b] = x
            continue
        xr, xi = x[:nb], x[nb:]
        y = jnp.concatenate([xr * hr - xi * hi, xr * hi + xi * hr], axis=0)
        yh, yl = _split(y)
        z = _dot3(gh_ref[...], gl_ref[...], yh, yl)
        o_ref[b, 0] = z[:nb]
        o_ref[b, 1] = z[nb:]


def _fft2_spectrum(a, fk):
    Bf, _, Na, Nb, C = a.shape
    mat = pl.BlockSpec((None, 2 * Nb, 2 * Nb), lambda k: (k, 0, 0))
    return pl.pallas_call(
        functools.partial(_fft2_kernel, False, Bf),
        grid=(Na,),
        in_specs=[mat, mat, pl.BlockSpec((Bf, 2, None, Nb, C), lambda k: (0, 0, k, 0, 0))],
        out_specs=pl.BlockSpec((Bf, None, 2 * Nb, C), lambda k: (0, k, 0, 0)),
        out_shape=jax.ShapeDtypeStruct((Bf, Na, 2 * Nb, C), F32),
        compiler_params=_cparams(("parallel",)),
    )(fk[0], fk[1], a)


def _fft2_conv(a, fk, gk, hspec):
    B, _, Na, Nb, C = a.shape
    mat = pl.BlockSpec((None, 2 * Nb, 2 * Nb), lambda k: (k, 0, 0))
    blk = pl.BlockSpec((B, 2, None, Nb, C), lambda k: (0, 0, k, 0, 0))
    return pl.pallas_call(
        functools.partial(_fft2_kernel, True, B),
        grid=(Na,),
        in_specs=[mat, mat, mat, mat, pl.BlockSpec((None, 2 * Nb, C), lambda k: (k, 0, 0)), blk],
        out_specs=blk,
        out_shape=jax.ShapeDtypeStruct(a.shape, F32),
        compiler_params=_cparams(("parallel",)),
    )(fk[0], fk[1], gk[0], gk[1], hspec, a)


def _fft3_kernel(fh_ref, fl_ref, b_ref, u_ref, bias_ref, x_ref, o_ref):
    bh, bl = _split(b_ref[...])
    y = _dot3(fh_ref[...], fl_ref[...], bh, bl)
    o_ref[...] = (x_ref[...] * (y + u_ref[...] * bias_ref[...])).astype(o_ref.dtype)


def _fft3(bmat, f3, u2, bias_t, x2):
    B, M2, W = bmat.shape
    K = u2.shape[1]
    tn = min(W, 6144)
    tok = pl.BlockSpec((None, K, tn), lambda b, j: (b, 0, j))
    return pl.pallas_call(
        _fft3_kernel,
        grid=(B, W // tn),
        in_specs=[_const_spec((K, M2)), _const_spec((K, M2)),
                  pl.BlockSpec((None, M2, tn), lambda b, j: (b, 0, j)), tok,
                  pl.BlockSpec((1, tn), lambda b, j: (0, j)), tok],
        out_specs=tok,
        out_shape=jax.ShapeDtypeStruct(u2.shape, F32),
        compiler_params=_cparams(("parallel", "parallel")),
    )(f3[0], f3[1], bmat, u2, bias_t, x2)


def _angles(rows, cols, n):
    prod = (rows[:, None] * cols[None, :]) % n
    return prod.astype(F32) * (2.0 * math.pi / n)


def _fft_consts(L):
    N = 2 * L
    Nb = FFT_NB
    Na = N // Nb
    nk = Na // 2 + 1
    nkp = 8 * ((nk + 7) // 8)
    ia = jnp.arange(Na, dtype=jnp.int32)
    ib = jnp.arange(Nb, dtype=jnp.int32)
    ik = jnp.arange(nkp, dtype=jnp.int32)
    valid = (ik < nk).astype(F32)
    th1 = _angles(ik, ia, Na)
    f1 = jnp.concatenate([jnp.cos(th1) * valid[:, None], -jnp.sin(th1) * valid[:, None]], axis=0)
    kfull = ik[:, None] + Na * ib[None, :]
    th2 = ((kfull[:, :, None] * ib[None, None, :]) % N).astype(F32) * (2.0 * math.pi / N)
    c2, s2 = jnp.cos(th2), jnp.sin(th2)
    fk = jnp.concatenate([jnp.concatenate([c2, s2], axis=2),
                          jnp.concatenate([-s2, c2], axis=2)], axis=1)
    c2t, s2t = jnp.swapaxes(c2, 1, 2), jnp.swapaxes(s2, 1, 2)
    gk = jnp.concatenate([jnp.concatenate([c2t, -s2t], axis=2),
                          jnp.concatenate([s2t, c2t], axis=2)], axis=1)
    th3 = _angles(ia[:Na // 2], ik, Na)
    wgt = jnp.where((ik == 0) | (ik == Na // 2), 1.0, 2.0) * valid / N
    f3 = jnp.concatenate([jnp.cos(th3) * wgt[None, :], -jnp.sin(th3) * wgt[None, :]], axis=1)
    return dict(f1_half=_split(f1[:, :Na // 2]), f1_full=_split(f1), fk=_split(fk), gk=_split(gk),
                f3=_split(f3), Na=Na, Nb=Nb, nkp=nkp)


def _dense_kernel(conv, fh_ref, fl_ref, u_ref, *rest):
    u = u_ref[...]
    uh, ul = _split(u)
    x = _dot3(fh_ref[...], fl_ref[...], uh, ul)
    if not conv:
        rest[0][...] = x
        return
    gh_ref, gl_ref, h_ref, bias_ref, x_ref, o_ref = rest
    n = x.shape[0] // 2
    hsp = h_ref[...]
    xr, xi, hr, hi = x[:n], x[n:], hsp[:n], hsp[n:]
    y = jnp.concatenate([xr * hr - xi * hi, xr * hi + xi * hr], axis=0)
    yh, yl = _split(y)
    z = _dot3(gh_ref[...], gl_ref[...], yh, yl)
    o_ref[...] = (x_ref[...] * (z + u * bias_ref[...])).astype(o_ref.dtype)


def _dense_consts(L):
    N = 2 * L
    i_n = jnp.arange(N, dtype=jnp.int32)
    th = _angles(i_n, i_n, N)
    fd = jnp.concatenate([jnp.cos(th), -jnp.sin(th)], axis=0)
    tht = _angles(i_n[:L], i_n, N)
    fi = jnp.concatenate([jnp.cos(tht), -jnp.sin(tht)], axis=1) / N
    return dict(fd_half=_split(fd[:, :L]), fd_full=_split(fd), fi=_split(fi))


def _dense_spectrum(taps, fd):
    Bf, N, C = taps.shape
    return pl.pallas_call(
        functools.partial(_dense_kernel, False),
        grid=(Bf,),
        in_specs=[_const_spec(fd[0].shape), _const_spec(fd[0].shape),
                  pl.BlockSpec((None, N, C), lambda b: (b, 0, 0))],
        out_specs=pl.BlockSpec((None, 2 * N, C), lambda b: (b, 0, 0)),
        out_shape=jax.ShapeDtypeStruct((Bf, 2 * N, C), F32),
        compiler_params=_cparams(("parallel",)),
    )(fd[0], fd[1], taps)


def _dense_conv(u, fd, fi, hspec, bias, xg):
    B, L, C = u.shape
    tok = pl.BlockSpec((None, L, C), lambda b: (b, 0, 0))
    return pl.pallas_call(
        functools.partial(_dense_kernel, True),
        grid=(B,),
        in_specs=[_const_spec(fd[0].shape), _const_spec(fd[0].shape), tok,
                  _const_spec(fi[0].shape), _const_spec(fi[0].shape),
                  _const_spec(hspec.shape), _const_spec((1, C)), tok],
        out_specs=tok,
        out_shape=jax.ShapeDtypeStruct((B, L, C), F32),
        compiler_params=_cparams(("parallel",)),
    )(fd[0], fd[1], u, fi[0], fi[1], hspec, bias.reshape(1, C), xg)


def _hyena_filters(L, f_w1, f_b1, f_w2, f_b2, f_w3, f_freq, f_decay):
    hp = lax.Precision.HIGHEST
    t = jnp.arange(L, dtype=F32)[:, None]
    t_norm = t / L
    bands = jnp.arange(1, HY_BANDS + 1, dtype=F32)
    ang = (2.0 * math.pi / L) * t * bands
    feats = jnp.concatenate([t_norm, jnp.cos(ang), jnp.sin(ang)], axis=-1)
    z = jnp.sin(f_freq * (jnp.dot(feats, f_w1, precision=hp) + f_b1))
    z = jnp.sin(f_freq * (jnp.dot(z, f_w2, precision=hp) + f_b2))
    h = jnp.dot(z, f_w3, precision=hp) * jnp.exp(-t_norm * jnp.abs(f_decay))
    h = h.reshape(L, 2, HY_ORDER, HY_WIDTH)
    h = h / jnp.sum(jnp.abs(h), axis=(0, 1), keepdims=True)
    hf, hb = h[:, 0], h[:, 1]
    taps = jnp.concatenate([hf, jnp.zeros((1, HY_ORDER, HY_WIDTH), F32), hb[:0:-1]], axis=0)
    return jnp.transpose(taps, (1, 0, 2))


def _hyena(p_hy, conv_w, conv_b, filt, bias):
    B, L, _ = p_hy.shape
    C = HY_WIDTH
    taps = _hyena_filters(L, *filt)
    v, x1, x2 = _hy_dwconv(p_hy, conv_w, conv_b)
    gates = (x1, x2)
    u = v
    if 2 * L <= 1024:
        cst = _dense_consts(L)
        spec = _dense_spectrum(taps, cst["fd_full"])
        for o in range(HY_ORDER):
            u = _dense_conv(u, cst["fd_half"], cst["fi"], spec[o], bias[o], gates[o])
        return u
    cst = _fft_consts(L)
    Na, Nb, nkp = cst["Na"], cst["Nb"], cst["nkp"]
    W = Nb * C
    ta = _fft1(taps.reshape(HY_ORDER, Na, W), cst["f1_full"])
    spec = _fft2_spectrum(ta.reshape(HY_ORDER, 2, nkp, Nb, C), cst["fk"])
    for o in range(HY_ORDER):
        u2 = u.reshape(B, Na // 2, W)
        a = _fft1(u2, cst["f1_half"])
        bm = _fft2_conv(a.reshape(B, 2, nkp, Nb, C), cst["fk"], cst["gk"], spec[o])
        bias_t = jnp.tile(bias[o], Nb).reshape(1, W)
        u = _fft3(bm.reshape(B, 2 * nkp, W), cst["f3"], u2, bias_t,
                  gates[o].reshape(B, Na // 2, W)).reshape(B, L, C)
    return u


def _gelu_tanh(x):
    return 0.5 * x * (1.0 + jnp.tanh(math.sqrt(2.0 / math.pi) * (x + 0.044715 * (x * x * x))))


def _merge_kernel(x_ref, g1_ref, pg_ref, yhy_ref, ys5_ref, ps5_ref, d_ref, ymla_ref,
                  wglu_ref, whyr, ws5r, wmlar, wo_ref, o_ref):
    D = x_ref.shape[-1]
    ys = _gelu_tanh(d_ref[...] * ps5_ref[...].astype(F32) + ys5_ref[...])
    ag = _dot(ys.astype(BF16), wglu_ref[...])
    y_s5 = ag[:, :S5_WIDTH] * _sigmoid(ag[:, S5_WIDTH:])
    pg = pg_ref[...].astype(F32)
    merged = (_sigmoid(pg[:, :D]) * _dot(yhy_ref[...].astype(BF16), whyr[...])
              + _sigmoid(pg[:, D:2 * D]) * _dot(y_s5.astype(BF16), ws5r[...])
              + _sigmoid(pg[:, 2 * D:]) * _dot(ymla_ref[...], wmlar[...]))
    o_ref[...] = x_ref[...] + g1_ref[...] * _dot(merged.astype(BF16), wo_ref[...])


def _merge(x, g1, pg, y_hy, ys5, ps5, d_skip, y_mla, wts, tm):
    B, L, D = x.shape
    tok = lambda n: pl.BlockSpec((None, tm, n), lambda b, i: (b, i, 0))
    per_b = pl.BlockSpec((None, 1, D), lambda b, i: (b, 0, 0))
    ws = [wts[n] for n in ("w_glu", "w_br_hy", "w_br_s5", "w_br_mla", "w_o")]
    return pl.pallas_call(
        _merge_kernel,
        grid=(B, L // tm),
        in_specs=[tok(D), per_b, tok(3 * D), tok(HY_WIDTH), tok(S5_WIDTH), tok(S5_WIDTH),
                  _const_spec((1, S5_WIDTH)), tok(MLA_HEADS * MLA_V)] + [_const_spec(w.shape) for w in ws],
        out_specs=tok(D),
        out_shape=jax.ShapeDtypeStruct((B, L, D), F32),
        compiler_params=_cparams(("parallel", "parallel")),
    )(x, g1, pg, y_hy, ys5, ps5, d_skip.reshape(1, -1), y_mla, *ws)


def _ffn_up_kernel(nt, x_ref, xp_ref, xn_ref, ge_ref, sh_ref, w_ref, cw_ref, cb_ref, o_ref):
    i = pl.program_id(1)
    tm = x_ref.shape[0]
    hid = w_ref.shape[1] // 2
    x = jnp.concatenate([xp_ref[...], x_ref[...], xn_ref[...]], axis=0)
    hb = (_rms(x, ge_ref[...]) + sh_ref[...]).astype(BF16)
    a = _dot(hb, w_ref[...])
    row8 = lax.broadcasted_iota(jnp.int32, (8, 1), 0)
    top = jnp.where(jnp.logical_and(row8 == 7, i == 0), 0.0, a[0:8])
    bot = jnp.where(jnp.logical_and(row8 == 0, i == nt - 1), 0.0, a[tm + 8:tm + 16])
    a = jnp.concatenate([top, a[8:tm + 8], bot], axis=0)
    am = pltpu.roll(a, 1, 0)[8:tm + 8]
    ap = pltpu.roll(a, tm + 15, 0)[8:tm + 8]
    cw = cw_ref[...]
    c = am * cw[0:1] + a[8:tm + 8] * cw[1:2] + ap * cw[2:3] + cb_ref[...]
    o_ref[...] = (_silu(c[:, :hid]) * c[:, hid:]).astype(o_ref.dtype)


def _ffn_up(x, ge, sh, w_up, conv_w, conv_b, tm):
    B, L, D = x.shape
    n = w_up.shape[1]
    nt = L // tm
    r8 = tm // 8
    per_b = pl.BlockSpec((None, 1, D), lambda b, i: (b, 0, 0))
    return pl.pallas_call(
        functools.partial(_ffn_up_kernel, nt),
        grid=(B, nt),
        in_specs=[pl.BlockSpec((None, tm, D), lambda b, i: (b, i, 0)),
                  pl.BlockSpec((None, 8, D), lambda b, i: (b, jnp.maximum(i * r8 - 1, 0), 0)),
                  pl.BlockSpec((None, 8, D), lambda b, i: (b, jnp.minimum((i + 1) * r8, L // 8 - 1), 0)),
                  per_b, per_b, _const_spec(w_up.shape), _const_spec((3, n)), _const_spec((1, n))],
        out_specs=pl.BlockSpec((None, tm, n // 2), lambda b, i: (b, i, 0)),
        out_shape=jax.ShapeDtypeStruct((B, L, n // 2), BF16),
        compiler_params=_cparams(("parallel", "parallel")),
    )(x, x, x, ge, sh, w_up, conv_w, conv_b.reshape(1, n))


def _ffn_down_kernel(final, act_ref, wd_ref, x_ref, g2_ref, *rest):
    y = x_ref[...] + g2_ref[...] * _dot(act_ref[...], wd_ref[...])
    if final:
        fg_ref, o_ref = rest
        o_ref[...] = _rms(y, fg_ref[...])
    else:
        rest[0][...] = y


def _ffn_down(act, w_down, x, g2, final_g, tm):
    B, L, n = act.shape
    D = x.shape[-1]
    final = final_g is not None
    in_specs = [pl.BlockSpec((None, tm, n), lambda b, i: (b, i, 0)), _const_spec(w_down.shape),
                pl.BlockSpec((None, tm, D), lambda b, i: (b, i, 0)),
                pl.BlockSpec((None, 1, D), lambda b, i: (b, 0, 0))]
    args = [act, w_down, x, g2]
    if final:
        in_specs.append(_const_spec((1, D)))
        args.append(final_g.reshape(1, D))
    return pl.pallas_call(
        functools.partial(_ffn_down_kernel, final),
        grid=(B, L // tm),
        in_specs=in_specs,
        out_specs=pl.BlockSpec((None, tm, D), lambda b, i: (b, i, 0)),
        out_shape=jax.ShapeDtypeStruct((B, L, D), F32),
        compiler_params=_cparams(("parallel", "parallel")),
    )(*args)


def _rope_tables(L, rope):
    half = MLA_ROPE // 2
    cos = jnp.ones((L, HEAD_PAD), F32)
    sa = jnp.zeros((L, HEAD_PAD), F32)
    sb = jnp.zeros((L, HEAD_PAD), F32)
    if not rope:
        return cos, sa, sb
    rows = L // GRID_W
    row = jnp.repeat(jnp.arange(rows, dtype=F32), GRID_W)
    col = jnp.tile(jnp.arange(GRID_W, dtype=F32), rows)
    n_ax = MLA_ROPE // 4
    inv = ROPE_THETA ** (-jnp.arange(n_ax, dtype=F32) / n_ax)
    ang = jnp.concatenate([row[:, None] * inv, col[:, None] * inv], axis=-1)
    c, s = jnp.cos(ang), jnp.sin(ang)
    cos = cos.at[:, MLA_NOPE:MLA_NOPE + half].set(c).at[:, MLA_NOPE + half:MLA_NOPE + 2 * half].set(c)
    sa = sa.at[:, MLA_NOPE:MLA_NOPE + half].set(-s)
    sb = sb.at[:, MLA_NOPE + half:MLA_NOPE + 2 * half].set(s)
    return cos, sa, sb


def _layer_weights(w_in, g_q, w_uq, g_kv, w_ukv, s5_w_glu, w_br_hy, w_br_s5, w_br_mla, w_o):
    D = w_in.shape[0]
    H = MLA_HEADS
    c_kv = S5_WIDTH
    c_kr = c_kv + w_ukv.shape[0]
    c_q = c_kr + MLA_ROPE
    c_hy = c_q + w_uq.shape[0]
    c_gate = c_hy + (HY_ORDER + 1) * HY_WIDTH
    bf = lambda a: a.astype(BF16)
    w_kr = jnp.zeros((D, HEAD_PAD), F32).at[:, MLA_NOPE:MLA_NOPE + MLA_ROPE].set(w_in[:, c_kr:c_q])
    ukv = w_ukv.reshape(-1, H, MLA_NOPE + MLA_V)
    w_uk = jnp.zeros((ukv.shape[0], H, HEAD_PAD), F32).at[:, :, :MLA_NOPE].set(ukv[:, :, :MLA_NOPE])
    w_uv = jnp.zeros((ukv.shape[0], H, HEAD_PAD), F32).at[:, :, :MLA_V].set(ukv[:, :, MLA_NOPE:])
    uq = w_uq.reshape(-1, H, MLA_NOPE + MLA_ROPE)
    w_uqp = jnp.zeros((uq.shape[0], H, HEAD_PAD), F32).at[:, :, :MLA_NOPE + MLA_ROPE].set(uq)
    return dict(
        w_s5=bf(w_in[:, :c_kv]), w_kv=bf(w_in[:, c_kv:c_kr]), w_kr=bf(w_kr), w_q=bf(w_in[:, c_q:c_hy]),
        w_hy=bf(w_in[:, c_hy:c_gate]), w_gate=bf(w_in[:, c_gate:]),
        g_kv=g_kv.reshape(1, -1), g_q=g_q.reshape(1, -1),
        w_uk=bf(w_uk.reshape(-1, H * HEAD_PAD)), w_uv=bf(w_uv.reshape(-1, H * HEAD_PAD)),
        w_uq=bf(w_uqp.reshape(-1, H * HEAD_PAD)),
        w_glu=bf(s5_w_glu), w_br_hy=bf(w_br_hy), w_br_s5=bf(w_br_s5), w_br_mla=bf(w_br_mla), w_o=bf(w_o))


def _mixer(x, mods, n1g, wts, s5_mats, s5_h0, s5_d, hy_args, rope_tabs, ctx_kv, full, tm, tq, tk):
    sh1, sc1, g1 = mods
    ge = n1g * (1.0 + sc1)
    outs = _in_proj(x, ge, sh1, rope_tabs, wts, full, tm)
    p_s5, k, v = outs[:3]
    y_s5, hfin = _s5_scan(p_s5, s5_mats, s5_h0)
    if not full:
        return None, hfin, (k, v)
    q, p_hy, p_gate = outs[3:]
    kvs = [(k, v)] + ([ctx_kv] if ctx_kv is not None else [])
    tks = [tk] + ([ctx_kv[0].shape[2]] if ctx_kv is not None else [])
    y_mla = _attention(q, kvs, tq, tks)
    y_hy = _hyena(p_hy, *hy_args)
    x_new = _merge(x, g1, p_gate, y_hy, y_s5, p_s5, s5_d, y_mla, wts, tm)
    return x_new, hfin, (k, v)


def _ffn(x, mods, n2g, w_up, conv_w, conv_b, w_down, final_g, tm):
    sh2, sc2, g2 = mods
    act = _ffn_up(x, n2g * (1.0 + sc2), sh2, w_up, conv_w, conv_b, tm)
    return _ffn_down(act, w_down, x, g2, final_g, tm)


def kernel(x, c, ctx, c_ctx, w_mod, b_mod, norm1_g, norm2_g, w_in, hy_conv_w, hy_conv_b, hy_f_w1, hy_f_b1, hy_f_w2, hy_f_b2, hy_f_w3, hy_f_freq, hy_f_decay, hy_bias, s5_lam_re, s5_lam_im, s5_log_step, s5_b_re, s5_b_im, s5_c_re, s5_c_im, s5_d, s5_w_glu, mla_g_q, mla_w_uq, mla_g_kv, mla_w_ukv, w_br_hy, w_br_s5, w_br_mla, w_o, ffn_w_up, ffn_conv_w, ffn_conv_b, ffn_w_down, final_g):
    B, L, D = x.shape
    Lc = ctx.shape[1]
    depth = w_mod.shape[0]
    tm = min(256, L)
    tmc = min(256, Lc)
    tq = min(1024, L)
    tk = min(1024, L)
    rope_lat = _rope_tables(L, True)
    rope_ctx = _rope_tables(Lc, False)
    mrows = 8 * ((B + 1 + 7) // 8)
    c_all = jnp.zeros((mrows, D), F32).at[:B].set(c).at[B].set(c_ctx)
    zeros_h = (jnp.zeros((S5_GROUPS, B, 2 * S5_STATE), F32),) * 2
    xc = ctx
    for i in range(depth):
        ctx_out = i < depth - 1
        mod = _modulation(c_all, w_mod[i], b_mod[i])
        m_lat = [m[:, None, :] for m in jnp.split(mod[:B], 6, axis=-1)]
        m_ctx = [jnp.broadcast_to(m[None, None, :], (B, 1, D)) for m in jnp.split(mod[B], 6, axis=-1)]
        wts = _layer_weights(w_in[i], mla_g_q[i], mla_w_uq[i], mla_g_kv[i], mla_w_ukv[i], s5_w_glu[i],
                             w_br_hy[i], w_br_s5[i], w_br_mla[i], w_o[i])
        s5_mats = _s5_matrices(s5_lam_re[i], s5_lam_im[i], s5_log_step[i], s5_b_re[i], s5_b_im[i],
                               s5_c_re[i], s5_c_im[i])
        filt = (hy_f_w1[i], hy_f_b1[i], hy_f_w2[i], hy_f_b2[i], hy_f_w3[i], hy_f_freq[i], hy_f_decay[i])
        hy_args = (hy_conv_w[i], hy_conv_b[i], filt, hy_bias[i])
        n1g, n2g = norm1_g[i][None, None, :], norm2_g[i][None, None, :]
        w_up, w_down = ffn_w_up[i].astype(BF16), ffn_w_down[i].astype(BF16)

        xc_new, hc_fin, ctx_kv = _mixer(xc, m_ctx[0:3], n1g, wts, s5_mats, zeros_h, s5_d[i], hy_args,
                                        rope_ctx, None, ctx_out, tmc, tmc, tmc)
        x, _, _ = _mixer(x, m_lat[0:3], n1g, wts, s5_mats, hc_fin, s5_d[i], hy_args,
                         rope_lat, ctx_kv, True, tm, tq, tk)
        last = i == depth - 1
        x = _ffn(x, m_lat[3:6], n2g, w_up, ffn_conv_w[i], ffn_conv_b[i], w_down,
                 final_g if last else None, tm)
        if ctx_out:
            xc = _ffn(xc_new, m_ctx[3:6], n2g, w_up, ffn_conv_w[i], ffn_conv_b[i], w_down, None, tmc)
    return x
```

```python
import functools
import math

import jax
import jax.numpy as jnp
from jax import lax
from jax.experimental import pallas as pl
from jax.experimental.pallas import tpu as pltpu

F32 = jnp.float32
BF16 = jnp.bfloat16

EPS = 1e-6
GRID_W = 64
HY_WIDTH = 384
HY_ORDER = 2
HY_BANDS = 8
S5_WIDTH = 384
S5_GROUP = 16
S5_GROUPS = S5_WIDTH // S5_GROUP
S5_STATE = 64
S5_CHUNK = 128
MLA_HEADS = 8
MLA_NOPE = 64
MLA_ROPE = 32
MLA_V = 64
MLA_SCALE = (MLA_NOPE + MLA_ROPE) ** -0.5
LOG2E = math.log2(math.e)
ROPE_THETA = 10000.0
HEAD_PAD = 128
FFT_NB = 128
VMEM_LIMIT = 56 * 1024 * 1024


def _cparams(sem):
    return pltpu.CompilerParams(dimension_semantics=sem, vmem_limit_bytes=VMEM_LIMIT)


def _split(x):
    hi = x.astype(BF16)
    lo = (x - hi.astype(F32)).astype(BF16)
    return hi, lo


def _dot(a, b):
    return jnp.dot(a, b, preferred_element_type=F32)


def _dot3(ah, al, bh, bl):
    return _dot(ah, bh) + _dot(al, bh) + _dot(ah, bl)


def _sigmoid(x):
    return 1.0 / (1.0 + jnp.exp(-x))


def _silu(x):
    return x * _sigmoid(x)


def _rms(x, g):
    return x * lax.rsqrt(jnp.mean(x * x, axis=-1, keepdims=True) + EPS) * g


def _const_spec(shape):
    nd = len(shape)
    return pl.BlockSpec(shape, lambda *_: (0,) * nd)


def _mod_kernel(c_ref, w_ref, b_ref, o_ref):
    ch, cl = _split(_silu(c_ref[...]))
    wh, wl = _split(w_ref[...])
    o_ref[...] = _dot3(ch, cl, wh, wl) + b_ref[...]


def _modulation(c_all, w, b):
    m, d = c_all.shape
    n = w.shape[1]
    tn = 1024
    return pl.pallas_call(
        _mod_kernel,
        grid=(n // tn,),
        in_specs=[pl.BlockSpec((m, d), lambda j: (0, 0)),
                  pl.BlockSpec((d, tn), lambda j: (0, j)),
                  pl.BlockSpec((1, tn), lambda j: (0, j))],
        out_specs=pl.BlockSpec((m, tn), lambda j: (0, j)),
        out_shape=jax.ShapeDtypeStruct((m, n), F32),
        compiler_params=_cparams(("arbitrary",)),
    )(c_all, w, b.reshape(1, n))


def _rope(x, cos, sa, sb):
    return (x * cos + pltpu.roll(x, HEAD_PAD - MLA_ROPE // 2, 1) * sa
            + pltpu.roll(x, MLA_ROPE // 2, 1) * sb)


def _in_kernel(full, x_ref, ge_ref, sh_ref, cos_ref, sa_ref, sb_ref,
               w_s5, w_kv, w_kr, gkv_ref, w_uk, w_uv, *rest):
    if full:
        (w_q, gq_ref, w_uq, w_hy, w_gate,
         o_s5, o_k, o_v, o_q, o_hy, o_gate) = rest
    else:
        o_s5, o_k, o_v = rest
    hb = (_rms(x_ref[...], ge_ref[...]) + sh_ref[...]).astype(BF16)
    cos, sa, sb = cos_ref[...], sa_ref[...], sb_ref[...]
    o_s5[...] = _dot(hb, w_s5[...]).astype(o_s5.dtype)
    nkv = _rms(_dot(hb, w_kv[...]), gkv_ref[...]).astype(BF16)
    vfull = _dot(nkv, w_uv[...])
    knope = _dot(nkv, w_uk[...])
    kr = _rope(_dot(hb, w_kr[...]), cos, sa, sb)
    ones_col = (lax.broadcasted_iota(jnp.int32, (1, HEAD_PAD), 1) == MLA_V).astype(F32)
    for h in range(MLA_HEADS):
        o_k[h] = (knope[:, h * HEAD_PAD:(h + 1) * HEAD_PAD] + kr).astype(o_k.dtype)
        o_v[h] = (vfull[:, h * HEAD_PAD:(h + 1) * HEAD_PAD] + ones_col).astype(o_v.dtype)
    if full:
        nq = _rms(_dot(hb, w_q[...]), gq_ref[...]).astype(BF16)
        qf = _dot(nq, w_uq[...])
        for h in range(MLA_HEADS):
            qh = _rope(qf[:, h * HEAD_PAD:(h + 1) * HEAD_PAD], cos, sa, sb) * (MLA_SCALE * LOG2E)
            o_q[h] = qh.astype(o_q.dtype)
        o_hy[...] = _dot(hb, w_hy[...]).astype(o_hy.dtype)
        o_gate[...] = _dot(hb, w_gate[...]).astype(o_gate.dtype)


def _in_proj(x, ge, sh, rope_tabs, wts, full, tm):
    B, L, D = x.shape
    H = MLA_HEADS
    tok = lambda n: pl.BlockSpec((None, tm, n), lambda b, i: (b, i, 0))
    per_b = pl.BlockSpec((None, 1, D), lambda b, i: (b, 0, 0))
    tab = pl.BlockSpec((tm, HEAD_PAD), lambda b, i: (i, 0))
    head = pl.BlockSpec((None, H, tm, HEAD_PAD), lambda b, i: (b, 0, i, 0))
    names = ["w_s5", "w_kv", "w_kr", "g_kv", "w_uk", "w_uv"]
    if full:
        names += ["w_q", "g_q", "w_uq", "w_hy", "w_gate"]
    ws = [wts[n] for n in names]
    in_specs = [tok(D), per_b, per_b, tab, tab, tab] + [_const_spec(w.shape) for w in ws]
    out_specs = [tok(S5_WIDTH), head, head]
    out_shape = [jax.ShapeDtypeStruct((B, L, S5_WIDTH), BF16),
                 jax.ShapeDtypeStruct((B, H, L, HEAD_PAD), BF16),
                 jax.ShapeDtypeStruct((B, H, L, HEAD_PAD), BF16)]
    if full:
        n_hy = (HY_ORDER + 1) * HY_WIDTH
        out_specs += [head, tok(n_hy), tok(3 * D)]
        out_shape += [jax.ShapeDtypeStruct((B, H, L, HEAD_PAD), BF16),
                      jax.ShapeDtypeStruct((B, L, n_hy), BF16),
                      jax.ShapeDtypeStruct((B, L, 3 * D), BF16)]
    return pl.pallas_call(
        functools.partial(_in_kernel, full),
        grid=(B, L // tm),
        in_specs=in_specs, out_specs=out_specs, out_shape=out_shape,
        compiler_params=_cparams(("parallel", "parallel")),
    )(x, ge, sh, *rope_tabs, *ws)


def _attn_kernel(segs, tq, q_ref, *refs):
    o_ref = refs[-1]
    nh = 2
    nsp = 1
    tr = tq // nsp
    chains = [(hh, sp) for hh in range(nh) for sp in range(nsp)]
    qs = [q_ref[hh, sp * tr:(sp + 1) * tr, :] for hh, sp in chains]
    carry = tuple((jnp.full((tr, 1), -jnp.inf, F32), jnp.zeros((tr, HEAD_PAD), F32)) for _ in chains)
    for si, (lk, tk) in enumerate(segs):
        k_ref, v_ref = refs[2 * si], refs[2 * si + 1]

        def body(j, c, k_ref=k_ref, v_ref=v_ref, tk=tk):
            off = pl.multiple_of(j * tk, tk)
            out = []
            for ci, (hh, _) in enumerate(chains):
                m, acc = c[ci]
                kb = k_ref[hh, pl.ds(off, tk), :]
                s = lax.dot_general(qs[ci], kb, (((1,), (1,)), ((), ())), preferred_element_type=F32)
                m_new = jnp.maximum(m, jnp.max(s, axis=-1, keepdims=True))
                p = jnp.exp2(s - m_new).astype(BF16)
                acc = jnp.exp2(m - m_new) * acc + _dot(p, v_ref[hh, pl.ds(off, tk), :])
                out.append((m_new, acc))
            return tuple(out)

        carry = lax.fori_loop(0, lk // tk, body, carry)
    outs = [acc[:, :MLA_V] / acc[:, MLA_V:MLA_V + 1] for _, acc in carry]
    heads = [jnp.concatenate(outs[hh * nsp:(hh + 1) * nsp], axis=0) for hh in range(nh)]
    o_ref[...] = jnp.concatenate(heads, axis=1).astype(o_ref.dtype)


def _attention(q, kvs, tq, tks):
    B, H, Lq, _ = q.shape
    segs = tuple((k.shape[2], tk) for (k, _), tk in zip(kvs, tks))
    in_specs = [pl.BlockSpec((None, 2, tq, HEAD_PAD), lambda b, j, i: (b, j, i, 0))]
    args = [q]
    for k, v in kvs:
        lk = k.shape[2]
        in_specs += [pl.BlockSpec((None, 2, lk, HEAD_PAD), lambda b, j, i: (b, j, 0, 0))] * 2
        args += [k, v]
    return pl.pallas_call(
        functools.partial(_attn_kernel, segs, tq),
        grid=(B, H // 2, Lq // tq),
        in_specs=in_specs,
        out_specs=pl.BlockSpec((None, tq, 2 * MLA_V), lambda b, j, i: (b, i, j)),
        out_shape=jax.ShapeDtypeStruct((B, Lq, H * MLA_V), BF16),
        compiler_params=_cparams(("parallel", "parallel", "arbitrary")),
    )(*args)


def _s5_pack_kernel(cp, p_ref, o_ref):
    x = p_ref[...].astype(F32)
    for kk in range(cp):
        o_ref[kk] = x[kk * S5_CHUNK:(kk + 1) * S5_CHUNK, :].T


def _s5_pack(p):
    B, L, C = p.shape
    T = S5_CHUNK
    nch = L // T
    cp = min(8, nch)
    return pl.pallas_call(
        functools.partial(_s5_pack_kernel, cp),
        grid=(B, nch // cp),
        in_specs=[pl.BlockSpec((None, cp * T, C), lambda b, i: (b, i, 0))],
        out_specs=pl.BlockSpec((None, cp, C, T), lambda b, i: (b, i, 0, 0)),
        out_shape=jax.ShapeDtypeStruct((B, nch, C, T), F32),
        compiler_params=_cparams(("parallel", "parallel")),
    )(p)


def _s5_unpack_kernel(cp, z_ref, o_ref):
    for kk in range(cp):
        o_ref[kk * S5_CHUNK:(kk + 1) * S5_CHUNK, :] = z_ref[kk].T


def _s5_unpack(z):
    B, nch, C, T = z.shape
    cp = min(8, nch)
    return pl.pallas_call(
        functools.partial(_s5_unpack_kernel, cp),
        grid=(B, nch // cp),
        in_specs=[pl.BlockSpec((None, cp, C, T), lambda b, i: (b, i, 0, 0))],
        out_specs=pl.BlockSpec((None, cp * T, C), lambda b, i: (b, i, 0)),
        out_shape=jax.ShapeDtypeStruct((B, nch * T, C), F32),
        compiler_params=_cparams(("parallel", "parallel")),
    )(z)


def _toeplitz_kernel(w_ref, o_ref):
    T, cg = S5_CHUNK, S5_GROUP

    def body(cp, carry):
        r0 = pl.multiple_of(cp * T, T)
        for c in range(cg):
            w = w_ref[pl.ds(cp * cg + c, 1), :]
            blk = pltpu.roll(jnp.broadcast_to(w, (T, 2 * T)), 0, 1, stride=1, stride_axis=0)
            o_ref[pl.ds(r0, T), c * T:(c + 1) * T] = blk[:, :T].astype(BF16)
        return carry

    lax.fori_loop(0, cg, body, 0)


def _toeplitz(w):
    G, npair, W = w.shape
    n = S5_GROUP * S5_CHUNK
    return pl.pallas_call(
        _toeplitz_kernel,
        grid=(G,),
        in_specs=[pl.BlockSpec((None, npair, W), lambda g: (g, 0, 0))],
        out_specs=pl.BlockSpec((None, n, n), lambda g: (g, 0, 0)),
        out_shape=jax.ShapeDtypeStruct((G, n, n), BF16),
        compiler_params=_cparams(("parallel",)),
    )(w)


def _s5_kernel(nch, nb, u_ref, mi_ref, min_ref, mout_ref, atr_ref, ati_ref, h0r_ref, h0i_ref,
               y_ref, hfr_ref, hfi_ref, sr_ref, si_ref, hr_ref, hi_ref):
    ns, cg, T = S5_STATE, S5_GROUP, S5_CHUNK
    u = jnp.concatenate([u_ref[:, c, :] for c in range(cg)], axis=1).astype(BF16)
    s = _dot(u, min_ref[...])
    sr_ref[...] = s[:, :2 * ns]
    si_ref[...] = s[:, 2 * ns:]
    atr, ati = atr_ref[...], ati_ref[...]
    fwd = lax.broadcasted_iota(jnp.int32, (1, 2 * ns), 1) < ns

    def body(i, carry):
        out = []
        for b in range(nb):
            hr, hi = carry[2 * b], carry[2 * b + 1]
            rf = b * nch + i
            rb = b * nch + (nch - 1 - i)
            hr_ref[pl.ds(rf, 1), 0:ns] = hr[:, 0:ns]
            hi_ref[pl.ds(rf, 1), 0:ns] = hi[:, 0:ns]
            hr_ref[pl.ds(rb, 1), ns:2 * ns] = hr[:, ns:2 * ns]
            hi_ref[pl.ds(rb, 1), ns:2 * ns] = hi[:, ns:2 * ns]
            s_r = jnp.where(fwd, sr_ref[pl.ds(rf, 1), :], sr_ref[pl.ds(rb, 1), :])
            s_i = jnp.where(fwd, si_ref[pl.ds(rf, 1), :], si_ref[pl.ds(rb, 1), :])
            out += [atr * hr - ati * hi + s_r, atr * hi + ati * hr + s_i]
        return tuple(out)

    init = []
    for b in range(nb):
        init += [h0r_ref[b:b + 1, :], h0i_ref[b:b + 1, :]]
    fin = lax.fori_loop(0, nch, body, tuple(init))
    for b in range(nb):
        hfr_ref[b:b + 1, :] = fin[2 * b]
        hfi_ref[b:b + 1, :] = fin[2 * b + 1]
    hcat = jnp.concatenate([hr_ref[...], hi_ref[...]], axis=1).astype(BF16)
    cb = 4
    for j in range(cg // cb):
        cols = slice(j * cb * T, (j + 1) * cb * T)
        y = _dot(u, mi_ref[:, cols]) + _dot(hcat, mout_ref[:, cols])
        for c in range(cb):
            y_ref[:, j * cb + c, :] = y[:, c * T:(c + 1) * T]


def _cmul(ar, ai, br, bi):
    return ar * br - ai * bi, ar * bi + ai * br


def _s5_matrices(lam_re, lam_im, log_step, b_re, b_im, c_re, c_im):
    hp = lax.Precision.HIGHEST
    T = S5_CHUNK
    step = jnp.exp(log_step)[..., None]
    def apow(tau):
        mag = jnp.exp(lam_re[:, :, None, :] * step[:, :, None, :] * tau[None, None, :, None])
        ang = lam_im[:, :, None, :] * step[:, :, None, :] * tau[None, None, :, None]
        return mag * jnp.cos(ang), mag * jnp.sin(ang)
    a_r, a_i = apow(jnp.ones((1,), F32))
    a_r, a_i = a_r[:, :, 0], a_i[:, :, 0]
    den = lam_re ** 2 + lam_im ** 2
    q_r, q_i = _cmul(a_r - 1.0, a_i, lam_re / den, -lam_im / den)
    bb_r, bb_i = _cmul(q_r[..., None], q_i[..., None], b_re, b_im)
    tt = jnp.arange(T, dtype=F32)
    p_r, p_i = apow(tt)
    cb_r, cb_i = _cmul(jnp.swapaxes(c_re, -1, -2)[..., :, :, None], jnp.swapaxes(c_im, -1, -2)[..., :, :, None],
                       bb_r[..., :, None, :], bb_i[..., :, None, :])
    kk = (jnp.einsum('dgtn,dgncx->dgtcx', p_r, cb_r, precision=hp)
          - jnp.einsum('dgtn,dgncx->dgtcx', p_i, cb_i, precision=hp))
    kf, kb = kk[0], kk[1]
    G, cg = kf.shape[0], kf.shape[-1]
    wlag = jnp.concatenate([kf[:, :1] + kb[:, :1], kf[:, 1:], jnp.zeros_like(kf[:, :1]), kb[:, :0:-1]], axis=1)
    wlag = jnp.transpose(wlag, (0, 3, 2, 1)).reshape(G, cg * cg, 2 * T)
    m_intra = _toeplitz(wlag)
    def instate(pr, pi, br, bi):
        return _cmul(pr[:, None, :, :], pi[:, None, :, :],
                     jnp.swapaxes(br, -1, -2)[:, :, None, :], jnp.swapaxes(bi, -1, -2)[:, :, None, :])
    f_r, f_i = instate(p_r[0][:, ::-1], p_i[0][:, ::-1], bb_r[0], bb_i[0])
    g_r, g_i = instate(p_r[1], p_i[1], bb_r[1], bb_i[1])
    m_in = jnp.concatenate([f_r, g_r, f_i, g_i], axis=-1).reshape(G, cg * T, 4 * S5_STATE)
    p1_r, p1_i = apow(tt + 1.0)
    def outstate(pr, pi, cr, ci):
        xr, xi = _cmul(jnp.swapaxes(pr, 1, 2)[:, :, None, :], jnp.swapaxes(pi, 1, 2)[:, :, None, :],
                       jnp.swapaxes(cr, 1, 2)[:, :, :, None], jnp.swapaxes(ci, 1, 2)[:, :, :, None])
        return xr, -xi
    of_r, of_i = outstate(p1_r[0], p1_i[0], c_re[0], c_im[0])
    ob_r, ob_i = outstate(p1_r[1][:, ::-1], p1_i[1][:, ::-1], c_re[1], c_im[1])
    m_out = jnp.concatenate([of_r, ob_r, of_i, ob_i], axis=1).reshape(G, 4 * S5_STATE, cg * T)
    at_r, at_i = apow(jnp.full((1,), float(T), F32))
    at_r = jnp.concatenate([at_r[0, :, 0], at_r[1, :, 0]], axis=-1)[:, None, :]
    at_i = jnp.concatenate([at_i[0, :, 0], at_i[1, :, 0]], axis=-1)[:, None, :]
    return m_intra, m_in.astype(BF16), m_out.astype(BF16), at_r, at_i


def _s5_scan(u, mats, h0):
    B, L, C = u.shape
    T, G, cg, ns = S5_CHUNK, S5_GROUPS, S5_GROUP, S5_STATE
    nch = L // T
    R = nch * B
    m_intra, m_in, m_out, at_r, at_i = mats
    z = _s5_pack(u).reshape(R, C, T)
    gspec = lambda r, c: pl.BlockSpec((None, r, c), lambda g: (g, 0, 0))
    zspec = pl.BlockSpec((R, cg, T), lambda g: (0, g, 0))
    y, hfr, hfi = pl.pallas_call(
        functools.partial(_s5_kernel, nch, B),
        grid=(G,),
        in_specs=[zspec, gspec(T * cg, T * cg), gspec(T * cg, 4 * ns), gspec(4 * ns, T * cg),
                  gspec(1, 2 * ns), gspec(1, 2 * ns), gspec(B, 2 * ns), gspec(B, 2 * ns)],
        out_specs=[zspec, gspec(B, 2 * ns), gspec(B, 2 * ns)],
        out_shape=[jax.ShapeDtypeStruct((R, C, T), F32),
                   jax.ShapeDtypeStruct((G, B, 2 * ns), F32),
                   jax.ShapeDtypeStruct((G, B, 2 * ns), F32)],
        scratch_shapes=[pltpu.VMEM((R, 2 * ns), F32)] * 4,
        compiler_params=_cparams(("parallel",)),
    )(z, m_intra, m_in, m_out, at_r, at_i, h0[0], h0[1])
    return _s5_unpack(y.reshape(B, nch, C, T)), (hfr, hfi)


def _shift_rows(x, prev_row, next_row):
    n = x.shape[0]
    row = lax.broadcasted_iota(jnp.int32, x.shape, 0)
    xm = jnp.where(row == 0, prev_row, pltpu.roll(x, 1, 0))
    xp = jnp.where(row == n - 1, next_row, pltpu.roll(x, n - 1, 0))
    return xm, xp


def _dwconv_kernel(p_ref, w_ref, b_ref, o_ref):
    x = p_ref[...].astype(F32)
    w = w_ref[...]
    xm, xp = _shift_rows(x, 0.0, 0.0)
    o_ref[...] = xm * w[0:1] + x * w[1:2] + xp * w[2:3] + b_ref[...]


def _hy_dwconv(p, w, b):
    B, L, n = p.shape
    tc = 128
    nblk = HY_WIDTH // tc
    outs = []
    for part in range(n // HY_WIDTH):
        outs.append(pl.pallas_call(
            _dwconv_kernel,
            grid=(B, nblk),
            in_specs=[pl.BlockSpec((None, L, tc), lambda bb, j, part=part: (bb, 0, part * nblk + j)),
                      pl.BlockSpec((3, tc), lambda bb, j, part=part: (0, part * nblk + j)),
                      pl.BlockSpec((1, tc), lambda bb, j, part=part: (0, part * nblk + j))],
            out_specs=pl.BlockSpec((None, L, tc), lambda bb, j: (bb, 0, j)),
            out_shape=jax.ShapeDtypeStruct((B, L, HY_WIDTH), F32),
            compiler_params=_cparams(("parallel", "parallel")),
        )(p, w, b.reshape(1, n)))
    return outs


def _fft1_kernel(f_ref, u_ref, o_ref):
    o_ref[...] = _dot(f_ref[...], u_ref[...].astype(BF16)).astype(o_ref.dtype)


def _fft1(u2, f1):
    B, K, W = u2.shape
    M = f1.shape[0]
    tn = min(W, 6144)
    return pl.pallas_call(
        _fft1_kernel,
        grid=(B, W // tn),
        in_specs=[_const_spec((M, K)), pl.BlockSpec((None, K, tn), lambda b, j: (b, 0, j))],
        out_specs=pl.BlockSpec((None, M, tn), lambda b, j: (b, 0, j)),
        out_shape=jax.ShapeDtypeStruct((B, M, W), BF16),
        compiler_params=_cparams(("parallel", "parallel")),
    )(f1, u2)


def _fft2_kernel(conv, nbat, f_ref, *rest):
    nb = FFT_NB
    if conv:
        g_ref, h_ref, a_ref, o_ref = rest
        hsp = h_ref[...]
        hr, hi = hsp[:nb], hsp[nb:]
    else:
        a_ref, o_ref = rest
    for b in range(nbat):
        a = jnp.concatenate([a_ref[b, 0], a_ref[b, 1]], axis=0)
        x = _dot(f_ref[...], a)
        if not conv:
            o_ref[b] = x
            continue
        xr, xi = x[:nb], x[nb:]
        y = jnp.concatenate([xr * hr - xi * hi, xr * hi + xi * hr], axis=0)
        z = _dot(g_ref[...], y.astype(BF16)).astype(o_ref.dtype)
        o_ref[b, 0] = z[:nb]
        o_ref[b, 1] = z[nb:]


def _fft2_spectrum(a, fk):
    Bf, _, Na, Nb, C = a.shape
    mat = pl.BlockSpec((None, 2 * Nb, 2 * Nb), lambda k: (k, 0, 0))
    return pl.pallas_call(
        functools.partial(_fft2_kernel, False, Bf),
        grid=(Na,),
        in_specs=[mat, pl.BlockSpec((Bf, 2, None, Nb, C), lambda k: (0, 0, k, 0, 0))],
        out_specs=pl.BlockSpec((Bf, None, 2 * Nb, C), lambda k: (0, k, 0, 0)),
        out_shape=jax.ShapeDtypeStruct((Bf, Na, 2 * Nb, C), F32),
        compiler_params=_cparams(("parallel",)),
    )(fk, a)


def _fft2_conv(a, fk, gk, hspec):
    B, _, Na, Nb, C = a.shape
    mat = pl.BlockSpec((None, 2 * Nb, 2 * Nb), lambda k: (k, 0, 0))
    blk = pl.BlockSpec((B, 2, None, Nb, C), lambda k: (0, 0, k, 0, 0))
    return pl.pallas_call(
        functools.partial(_fft2_kernel, True, B),
        grid=(Na,),
        in_specs=[mat, mat, pl.BlockSpec((None, 2 * Nb, C), lambda k: (k, 0, 0)), blk],
        out_specs=blk,
        out_shape=jax.ShapeDtypeStruct(a.shape, BF16),
        compiler_params=_cparams(("parallel",)),
    )(fk, gk, hspec, a)


def _fft3_kernel(f_ref, b_ref, u_ref, bias_ref, x_ref, o_ref):
    y = _dot(f_ref[...], b_ref[...])
    o_ref[...] = (x_ref[...] * (y + u_ref[...] * bias_ref[...])).astype(o_ref.dtype)


def _fft3(bmat, f3, u2, bias_t, x2):
    B, M2, W = bmat.shape
    K = u2.shape[1]
    tn = min(W, 6144)
    tok = pl.BlockSpec((None, K, tn), lambda b, j: (b, 0, j))
    return pl.pallas_call(
        _fft3_kernel,
        grid=(B, W // tn),
        in_specs=[_const_spec((K, M2)), pl.BlockSpec((None, M2, tn), lambda b, j: (b, 0, j)), tok,
                  pl.BlockSpec((1, tn), lambda b, j: (0, j)), tok],
        out_specs=tok,
        out_shape=jax.ShapeDtypeStruct(u2.shape, F32),
        compiler_params=_cparams(("parallel", "parallel")),
    )(f3, bmat, u2, bias_t, x2)


def _angles(rows, cols, n):
    prod = (rows[:, None] * cols[None, :]) % n
    return prod.astype(F32) * (2.0 * math.pi / n)


def _fft_consts(L):
    N = 2 * L
    Nb = FFT_NB
    Na = N // Nb
    nk = Na // 2 + 1
    nkp = 8 * ((nk + 7) // 8)
    ia = jnp.arange(Na, dtype=jnp.int32)
    ib = jnp.arange(Nb, dtype=jnp.int32)
    ik = jnp.arange(nkp, dtype=jnp.int32)
    valid = (ik < nk).astype(F32)
    th1 = _angles(ik, ia, Na)
    f1 = jnp.concatenate([jnp.cos(th1) * valid[:, None], -jnp.sin(th1) * valid[:, None]], axis=0)
    kfull = ik[:, None] + Na * ib[None, :]
    th2 = ((kfull[:, :, None] * ib[None, None, :]) % N).astype(F32) * (2.0 * math.pi / N)
    c2, s2 = jnp.cos(th2), jnp.sin(th2)
    fk = jnp.concatenate([jnp.concatenate([c2, s2], axis=2),
                          jnp.concatenate([-s2, c2], axis=2)], axis=1)
    c2t, s2t = jnp.swapaxes(c2, 1, 2), jnp.swapaxes(s2, 1, 2)
    gk = jnp.concatenate([jnp.concatenate([c2t, -s2t], axis=2),
                          jnp.concatenate([s2t, c2t], axis=2)], axis=1)
    th3 = _angles(ia[:Na // 2], ik, Na)
    wgt = jnp.where((ik == 0) | (ik == Na // 2), 1.0, 2.0) * valid / N
    f3 = jnp.concatenate([jnp.cos(th3) * wgt[None, :], -jnp.sin(th3) * wgt[None, :]], axis=1)
    bf = lambda a: a.astype(BF16)
    return dict(f1_half=bf(f1[:, :Na // 2]), f1_full=bf(f1), fk=bf(fk), gk=bf(gk), f3=bf(f3),
                Na=Na, Nb=Nb, nkp=nkp)


def _dense_kernel(conv, f_ref, u_ref, *rest):
    u = u_ref[...]
    x = _dot(f_ref[...], u.astype(BF16))
    if not conv:
        rest[0][...] = x
        return
    g_ref, h_ref, bias_ref, x_ref, o_ref = rest
    n = x.shape[0] // 2
    hsp = h_ref[...]
    xr, xi, hr, hi = x[:n], x[n:], hsp[:n], hsp[n:]
    y = jnp.concatenate([xr * hr - xi * hi, xr * hi + xi * hr], axis=0)
    z = _dot(g_ref[...], y.astype(BF16))
    o_ref[...] = (x_ref[...] * (z + u * bias_ref[...])).astype(o_ref.dtype)


def _dense_consts(L):
    N = 2 * L
    i_n = jnp.arange(N, dtype=jnp.int32)
    th = _angles(i_n, i_n, N)
    fd = jnp.concatenate([jnp.cos(th), -jnp.sin(th)], axis=0)
    tht = _angles(i_n[:L], i_n, N)
    fi = jnp.concatenate([jnp.cos(tht), -jnp.sin(tht)], axis=1) / N
    return dict(fd_half=fd[:, :L].astype(BF16), fd_full=fd.astype(BF16), fi=fi.astype(BF16))


def _dense_spectrum(taps, fd):
    Bf, N, C = taps.shape
    return pl.pallas_call(
        functools.partial(_dense_kernel, False),
        grid=(Bf,),
        in_specs=[_const_spec(fd.shape), pl.BlockSpec((None, N, C), lambda b: (b, 0, 0))],
        out_specs=pl.BlockSpec((None, 2 * N, C), lambda b: (b, 0, 0)),
        out_shape=jax.ShapeDtypeStruct((Bf, 2 * N, C), F32),
        compiler_params=_cparams(("parallel",)),
    )(fd, taps)


def _dense_conv(u, fd, fi, hspec, bias, xg):
    B, L, C = u.shape
    tok = pl.BlockSpec((None, L, C), lambda b: (b, 0, 0))
    return pl.pallas_call(
        functools.partial(_dense_kernel, True),
        grid=(B,),
        in_specs=[_const_spec(fd.shape), tok, _const_spec(fi.shape),
                  _const_spec(hspec.shape), _const_spec((1, C)), tok],
        out_specs=tok,
        out_shape=jax.ShapeDtypeStruct((B, L, C), F32),
        compiler_params=_cparams(("parallel",)),
    )(fd, u, fi, hspec, bias.reshape(1, C), xg)


def _hyena_filters(L, f_w1, f_b1, f_w2, f_b2, f_w3, f_freq, f_decay):
    hp = lax.Precision.HIGHEST
    t = jnp.arange(L, dtype=F32)[:, None]
    t_norm = t / L
    bands = jnp.arange(1, HY_BANDS + 1, dtype=F32)
    ang = (2.0 * math.pi / L) * t * bands
    feats = jnp.concatenate([t_norm, jnp.cos(ang), jnp.sin(ang)], axis=-1)
    z = jnp.sin(f_freq * (jnp.dot(feats, f_w1, precision=hp) + f_b1))
    z = jnp.sin(f_freq * (jnp.dot(z, f_w2, precision=hp) + f_b2))
    h = jnp.dot(z, f_w3, precision=hp) * jnp.exp(-t_norm * jnp.abs(f_decay))
    h = h.reshape(L, 2, HY_ORDER, HY_WIDTH)
    h = h / jnp.sum(jnp.abs(h), axis=(0, 1), keepdims=True)
    hf, hb = h[:, 0], h[:, 1]
    taps = jnp.concatenate([hf, jnp.zeros((1, HY_ORDER, HY_WIDTH), F32), hb[:0:-1]], axis=0)
    return jnp.transpose(taps, (1, 0, 2))


def _hyena(p_hy, conv_w, conv_b, filt, bias):
    B, L, _ = p_hy.shape
    C = HY_WIDTH
    taps = _hyena_filters(L, *filt)
    v, x1, x2 = _hy_dwconv(p_hy, conv_w, conv_b)
    gates = (x1, x2)
    u = v
    if 2 * L <= 1024:
        cst = _dense_consts(L)
        spec = _dense_spectrum(taps, cst["fd_full"])
        for o in range(HY_ORDER):
            u = _dense_conv(u, cst["fd_half"], cst["fi"], spec[o], bias[o], gates[o])
        return u
    cst = _fft_consts(L)
    Na, Nb, nkp = cst["Na"], cst["Nb"], cst["nkp"]
    W = Nb * C
    ta = _fft1(taps.reshape(HY_ORDER, Na, W), cst["f1_full"])
    spec = _fft2_spectrum(ta.reshape(HY_ORDER, 2, nkp, Nb, C), cst["fk"])
    for o in range(HY_ORDER):
        u2 = u.reshape(B, Na // 2, W)
        a = _fft1(u2, cst["f1_half"])
        bm = _fft2_conv(a.reshape(B, 2, nkp, Nb, C), cst["fk"], cst["gk"], spec[o])
        bias_t = jnp.tile(bias[o], Nb).reshape(1, W)
        u = _fft3(bm.reshape(B, 2 * nkp, W), cst["f3"], u2, bias_t,
                  gates[o].reshape(B, Na // 2, W)).reshape(B, L, C)
    return u


def _gelu_tanh(x):
    return 0.5 * x * (1.0 + jnp.tanh(math.sqrt(2.0 / math.pi) * (x + 0.044715 * (x * x * x))))


def _merge_kernel(x_ref, g1_ref, pg_ref, yhy_ref, ys5_ref, ps5_ref, d_ref, ymla_ref,
                  wglu_ref, whyr, ws5r, wmlar, wo_ref, o_ref):
    D = x_ref.shape[-1]
    ys = _gelu_tanh(d_ref[...] * ps5_ref[...].astype(F32) + ys5_ref[...])
    ag = _dot(ys.astype(BF16), wglu_ref[...])
    y_s5 = ag[:, :S5_WIDTH] * _sigmoid(ag[:, S5_WIDTH:])
    pg = pg_ref[...].astype(F32)
    merged = (_sigmoid(pg[:, :D]) * _dot(yhy_ref[...].astype(BF16), whyr[...])
              + _sigmoid(pg[:, D:2 * D]) * _dot(y_s5.astype(BF16), ws5r[...])
              + _sigmoid(pg[:, 2 * D:]) * _dot(ymla_ref[...], wmlar[...]))
    o_ref[...] = x_ref[...] + g1_ref[...] * _dot(merged.astype(BF16), wo_ref[...])


def _merge(x, g1, pg, y_hy, ys5, ps5, d_skip, y_mla, wts, tm):
    B, L, D = x.shape
    tok = lambda n: pl.BlockSpec((None, tm, n), lambda b, i: (b, i, 0))
    per_b = pl.BlockSpec((None, 1, D), lambda b, i: (b, 0, 0))
    ws = [wts[n] for n in ("w_glu", "w_br_hy", "w_br_s5", "w_br_mla", "w_o")]
    return pl.pallas_call(
        _merge_kernel,
        grid=(B, L // tm),
        in_specs=[tok(D), per_b, tok(3 * D), tok(HY_WIDTH), tok(S5_WIDTH), tok(S5_WIDTH),
                  _const_spec((1, S5_WIDTH)), tok(MLA_HEADS * MLA_V)] + [_const_spec(w.shape) for w in ws],
        out_specs=tok(D),
        out_shape=jax.ShapeDtypeStruct((B, L, D), F32),
        compiler_params=_cparams(("parallel", "parallel")),
    )(x, g1, pg, y_hy, ys5, ps5, d_skip.reshape(1, -1), y_mla, *ws)


def _ffn_up_kernel(nt, x_ref, xp_ref, xn_ref, ge_ref, sh_ref, w_ref, cw_ref, cb_ref, o_ref):
    i = pl.program_id(1)
    tm = x_ref.shape[0]
    hid = w_ref.shape[1] // 2
    x = jnp.concatenate([xp_ref[...], x_ref[...], xn_ref[...]], axis=0)
    hb = (_rms(x, ge_ref[...]) + sh_ref[...]).astype(BF16)
    a = _dot(hb, w_ref[...])
    row8 = lax.broadcasted_iota(jnp.int32, (8, 1), 0)
    top = jnp.where(jnp.logical_and(row8 == 7, i == 0), 0.0, a[0:8])
    bot = jnp.where(jnp.logical_and(row8 == 0, i == nt - 1), 0.0, a[tm + 8:tm + 16])
    a = jnp.concatenate([top, a[8:tm + 8], bot], axis=0)
    am = pltpu.roll(a, 1, 0)[8:tm + 8]
    ap = pltpu.roll(a, tm + 15, 0)[8:tm + 8]
    cw = cw_ref[...]
    c = am * cw[0:1] + a[8:tm + 8] * cw[1:2] + ap * cw[2:3] + cb_ref[...]
    o_ref[...] = (_silu(c[:, :hid]) * c[:, hid:]).astype(o_ref.dtype)


def _ffn_up(x, ge, sh, w_up, conv_w, conv_b, tm):
    B, L, D = x.shape
    n = w_up.shape[1]
    nt = L // tm
    r8 = tm // 8
    per_b = pl.BlockSpec((None, 1, D), lambda b, i: (b, 0, 0))
    return pl.pallas_call(
        functools.partial(_ffn_up_kernel, nt),
        grid=(B, nt),
        in_specs=[pl.BlockSpec((None, tm, D), lambda b, i: (b, i, 0)),
                  pl.BlockSpec((None, 8, D), lambda b, i: (b, jnp.maximum(i * r8 - 1, 0), 0)),
                  pl.BlockSpec((None, 8, D), lambda b, i: (b, jnp.minimum((i + 1) * r8, L // 8 - 1), 0)),
                  per_b, per_b, _const_spec(w_up.shape), _const_spec((3, n)), _const_spec((1, n))],
        out_specs=pl.BlockSpec((None, tm, n // 2), lambda b, i: (b, i, 0)),
        out_shape=jax.ShapeDtypeStruct((B, L, n // 2), BF16),
        compiler_params=_cparams(("parallel", "parallel")),
    )(x, x, x, ge, sh, w_up, conv_w, conv_b.reshape(1, n))


def _ffn_down_kernel(final, act_ref, wd_ref, x_ref, g2_ref, *rest):
    y = x_ref[...] + g2_ref[...] * _dot(act_ref[...], wd_ref[...])
    if final:
        fg_ref, o_ref = rest
        o_ref[...] = _rms(y, fg_ref[...])
    else:
        rest[0][...] = y


def _ffn_down(act, w_down, x, g2, final_g, tm):
    B, L, n = act.shape
    D = x.shape[-1]
    final = final_g is not None
    in_specs = [pl.BlockSpec((None, tm, n), lambda b, i: (b, i, 0)), _const_spec(w_down.shape),
                pl.BlockSpec((None, tm, D), lambda b, i: (b, i, 0)),
                pl.BlockSpec((None, 1, D), lambda b, i: (b, 0, 0))]
    args = [act, w_down, x, g2]
    if final:
        in_specs.append(_const_spec((1, D)))
        args.append(final_g.reshape(1, D))
    return pl.pallas_call(
        functools.partial(_ffn_down_kernel, final),
        grid=(B, L // tm),
        in_specs=in_specs,
        out_specs=pl.BlockSpec((None, tm, D), lambda b, i: (b, i, 0)),
        out_shape=jax.ShapeDtypeStruct((B, L, D), F32),
        compiler_params=_cparams(("parallel", "parallel")),
    )(*args)


def _rope_tables(L, rope):
    half = MLA_ROPE // 2
    cos = jnp.ones((L, HEAD_PAD), F32)
    sa = jnp.zeros((L, HEAD_PAD), F32)
    sb = jnp.zeros((L, HEAD_PAD), F32)
    if not rope:
        return cos, sa, sb
    rows = L // GRID_W
    row = jnp.repeat(jnp.arange(rows, dtype=F32), GRID_W)
    col = jnp.tile(jnp.arange(GRID_W, dtype=F32), rows)
    n_ax = MLA_ROPE // 4
    inv = ROPE_THETA ** (-jnp.arange(n_ax, dtype=F32) / n_ax)
    ang = jnp.concatenate([row[:, None] * inv, col[:, None] * inv], axis=-1)
    c, s = jnp.cos(ang), jnp.sin(ang)
    cos = cos.at[:, MLA_NOPE:MLA_NOPE + half].set(c).at[:, MLA_NOPE + half:MLA_NOPE + 2 * half].set(c)
    sa = sa.at[:, MLA_NOPE:MLA_NOPE + half].set(-s)
    sb = sb.at[:, MLA_NOPE + half:MLA_NOPE + 2 * half].set(s)
    return cos, sa, sb


def _layer_weights(w_in, g_q, w_uq, g_kv, w_ukv, s5_w_glu, w_br_hy, w_br_s5, w_br_mla, w_o):
    D = w_in.shape[0]
    H = MLA_HEADS
    c_kv = S5_WIDTH
    c_kr = c_kv + w_ukv.shape[0]
    c_q = c_kr + MLA_ROPE
    c_hy = c_q + w_uq.shape[0]
    c_gate = c_hy + (HY_ORDER + 1) * HY_WIDTH
    bf = lambda a: a.astype(BF16)
    w_kr = jnp.zeros((D, HEAD_PAD), F32).at[:, MLA_NOPE:MLA_NOPE + MLA_ROPE].set(w_in[:, c_kr:c_q])
    ukv = w_ukv.reshape(-1, H, MLA_NOPE + MLA_V)
    w_uk = jnp.zeros((ukv.shape[0], H, HEAD_PAD), F32).at[:, :, :MLA_NOPE].set(ukv[:, :, :MLA_NOPE])
    w_uv = jnp.zeros((ukv.shape[0], H, HEAD_PAD), F32).at[:, :, :MLA_V].set(ukv[:, :, MLA_NOPE:])
    uq = w_uq.reshape(-1, H, MLA_NOPE + MLA_ROPE)
    w_uqp = jnp.zeros((uq.shape[0], H, HEAD_PAD), F32).at[:, :, :MLA_NOPE + MLA_ROPE].set(uq)
    return dict(
        w_s5=bf(w_in[:, :c_kv]), w_kv=bf(w_in[:, c_kv:c_kr]), w_kr=bf(w_kr), w_q=bf(w_in[:, c_q:c_hy]),
        w_hy=bf(w_in[:, c_hy:c_gate]), w_gate=bf(w_in[:, c_gate:]),
        g_kv=g_kv.reshape(1, -1), g_q=g_q.reshape(1, -1),
        w_uk=bf(w_uk.reshape(-1, H * HEAD_PAD)), w_uv=bf(w_uv.reshape(-1, H * HEAD_PAD)),
        w_uq=bf(w_uqp.reshape(-1, H * HEAD_PAD)),
        w_glu=bf(s5_w_glu), w_br_hy=bf(w_br_hy), w_br_s5=bf(w_br_s5), w_br_mla=bf(w_br_mla), w_o=bf(w_o))


def _mixer(x, mods, n1g, wts, s5_mats, s5_h0, s5_d, hy_args, rope_tabs, ctx_kv, full, tm, tq, tk):
    sh1, sc1, g1 = mods
    ge = n1g * (1.0 + sc1)
    outs = _in_proj(x, ge, sh1, rope_tabs, wts, full, tm)
    p_s5, k, v = outs[:3]
    y_s5, hfin = _s5_scan(p_s5, s5_mats, s5_h0)
    if not full:
        return None, hfin, (k, v)
    q, p_hy, p_gate = outs[3:]
    kvs = [(k, v)] + ([ctx_kv] if ctx_kv is not None else [])
    tks = [tk] + ([ctx_kv[0].shape[2]] if ctx_kv is not None else [])
    y_mla = _attention(q, kvs, tq, tks)
    y_hy = _hyena(p_hy, *hy_args)
    x_new = _merge(x, g1, p_gate, y_hy, y_s5, p_s5, s5_d, y_mla, wts, tm)
    return x_new, hfin, (k, v)


def _ffn(x, mods, n2g, w_up, conv_w, conv_b, w_down, final_g, tm):
    sh2, sc2, g2 = mods
    act = _ffn_up(x, n2g * (1.0 + sc2), sh2, w_up, conv_w, conv_b, tm)
    return _ffn_down(act, w_down, x, g2, final_g, tm)


def kernel(x, c, ctx, c_ctx, w_mod, b_mod, norm1_g, norm2_g, w_in, hy_conv_w, hy_conv_b, hy_f_w1, hy_f_b1, hy_f_w2, hy_f_b2, hy_f_w3, hy_f_freq, hy_f_decay, hy_bias, s5_lam_re, s5_lam_im, s5_log_step, s5_b_re, s5_b_im, s5_c_re, s5_c_im, s5_d, s5_w_glu, mla_g_q, mla_w_uq, mla_g_kv, mla_w_ukv, w_br_hy, w_br_s5, w_br_mla, w_o, ffn_w_up, ffn_conv_w, ffn_conv_b, ffn_w_down, final_g):
    B, L, D = x.shape
    Lc = ctx.shape[1]
    depth = w_mod.shape[0]
    tm = min(256, L)
    tmc = min(256, Lc)
    tq = min(1024, L)
    tk = min(2048, L)
    rope_lat = _rope_tables(L, True)
    rope_ctx = _rope_tables(Lc, False)
    mrows = 8 * ((B + 1 + 7) // 8)
    c_all = jnp.zeros((mrows, D), F32).at[:B].set(c).at[B].set(c_ctx)
    zeros_h = (jnp.zeros((S5_GROUPS, B, 2 * S5_STATE), F32),) * 2
    xc = ctx
    for i in range(depth):
        ctx_out = i < depth - 1
        mod = _modulation(c_all, w_mod[i], b_mod[i])
        m_lat = [m[:, None, :] for m in jnp.split(mod[:B], 6, axis=-1)]
        m_ctx = [jnp.broadcast_to(m[None, None, :], (B, 1, D)) for m in jnp.split(mod[B], 6, axis=-1)]
        wts = _layer_weights(w_in[i], mla_g_q[i], mla_w_uq[i], mla_g_kv[i], mla_w_ukv[i], s5_w_glu[i],
                             w_br_hy[i], w_br_s5[i], w_br_mla[i], w_o[i])
        s5_mats = _s5_matrices(s5_lam_re[i], s5_lam_im[i], s5_log_step[i], s5_b_re[i], s5_b_im[i],
                               s5_c_re[i], s5_c_im[i])
        filt = (hy_f_w1[i], hy_f_b1[i], hy_f_w2[i], hy_f_b2[i], hy_f_w3[i], hy_f_freq[i], hy_f_decay[i])
        hy_args = (hy_conv_w[i], hy_conv_b[i], filt, hy_bias[i])
        n1g, n2g = norm1_g[i][None, None, :], norm2_g[i][None, None, :]
        w_up, w_down = ffn_w_up[i].astype(BF16), ffn_w_down[i].astype(BF16)

        xc_new, hc_fin, ctx_kv = _mixer(xc, m_ctx[0:3], n1g, wts, s5_mats, zeros_h, s5_d[i], hy_args,
                                        rope_ctx, None, ctx_out, tmc, tmc, tmc)
        x, _, _ = _mixer(x, m_lat[0:3], n1g, wts, s5_mats, hc_fin, s5_d[i], hy_args,
                         rope_lat, ctx_kv, True, tm, tq, tk)
        last = i == depth - 1
        x = _ffn(x, m_lat[3:6], n2g, w_up, ffn_conv_w[i], ffn_conv_b[i], w_down,
                 final_g if last else None, tm)
        if ctx_out:
            xc = _ffn(xc_new, m_ctx[3:6], n2g, w_up, ffn_conv_w[i], ffn_conv_b[i], w_down, None, tmc)
    return x
```

```python
import functools
import math

import jax
import jax.numpy as jnp
from jax import lax
from jax.experimental import pallas as pl
from jax.experimental.pallas import tpu as pltpu

F32 = jnp.float32
BF16 = jnp.bfloat16

EPS = 1e-6
GRID_W = 64
HY_WIDTH = 384
HY_ORDER = 2
HY_BANDS = 8
S5_WIDTH = 384
S5_GROUP = 16
S5_GROUPS = S5_WIDTH // S5_GROUP
S5_STATE = 64
S5_CHUNK = 128
MLA_HEADS = 8
MLA_NOPE = 64
MLA_ROPE = 32
MLA_V = 64
MLA_SCALE = (MLA_NOPE + MLA_ROPE) ** -0.5
LOG2E = math.log2(math.e)
ROPE_THETA = 10000.0
HEAD_PAD = 128
FFT_NB = 128
VMEM_LIMIT = 56 * 1024 * 1024


def _cparams(sem):
    return pltpu.CompilerParams(dimension_semantics=sem, vmem_limit_bytes=VMEM_LIMIT)


def _split(x):
    hi = x.astype(BF16)
    lo = (x - hi.astype(F32)).astype(BF16)
    return hi, lo


def _dot(a, b):
    return jnp.dot(a, b, preferred_element_type=F32)


def _dot3(ah, al, bh, bl):
    return _dot(ah, bh) + _dot(al, bh) + _dot(ah, bl)


def _sigmoid(x):
    return 1.0 / (1.0 + jnp.exp(-x))


def _silu(x):
    return x * _sigmoid(x)


def _rms(x, g):
    return x * lax.rsqrt(jnp.mean(x * x, axis=-1, keepdims=True) + EPS) * g


def _const_spec(shape):
    nd = len(shape)
    return pl.BlockSpec(shape, lambda *_: (0,) * nd, pipeline_mode=pl.Buffered(1))


def _mod_kernel(c_ref, w_ref, b_ref, o_ref):
    ch, cl = _split(_silu(c_ref[...]))
    wh, wl = _split(w_ref[...])
    o_ref[...] = _dot3(ch, cl, wh, wl) + b_ref[...]


def _modulation(c_all, w, b):
    m, d = c_all.shape
    n = w.shape[1]
    tn = 1024
    return pl.pallas_call(
        _mod_kernel,
        grid=(n // tn,),
        in_specs=[pl.BlockSpec((m, d), lambda j: (0, 0)),
                  pl.BlockSpec((d, tn), lambda j: (0, j)),
                  pl.BlockSpec((1, tn), lambda j: (0, j))],
        out_specs=pl.BlockSpec((m, tn), lambda j: (0, j)),
        out_shape=jax.ShapeDtypeStruct((m, n), F32),
        compiler_params=_cparams(("arbitrary",)),
    )(c_all, w, b.reshape(1, n))


def _rope(x, cos, sa, sb):
    return (x * cos + pltpu.roll(x, HEAD_PAD - MLA_ROPE // 2, 1) * sa
            + pltpu.roll(x, MLA_ROPE // 2, 1) * sb)


def _in_kernel(full, x_ref, ge_ref, sh_ref, cos_ref, sa_ref, sb_ref,
               w_s5, w_kv, w_kr, gkv_ref, w_uk, w_uv, *rest):
    if full:
        (w_q, gq_ref, w_uq, w_hy, w_gate,
         o_s5, o_k, o_v, o_q, o_hy, o_gate) = rest
    else:
        o_s5, o_k, o_v = rest
    hb = (_rms(x_ref[...], ge_ref[...]) + sh_ref[...]).astype(BF16)
    cos, sa, sb = cos_ref[...], sa_ref[...], sb_ref[...]
    o_s5[...] = _dot(hb, w_s5[...]).astype(o_s5.dtype)
    nkv = _rms(_dot(hb, w_kv[...]), gkv_ref[...]).astype(BF16)
    vfull = _dot(nkv, w_uv[...])
    knope = _dot(nkv, w_uk[...])
    kr = _rope(_dot(hb, w_kr[...]), cos, sa, sb)
    ones_col = (lax.broadcasted_iota(jnp.int32, (1, HEAD_PAD), 1) == MLA_V).astype(F32)
    for h in range(MLA_HEADS):
        o_k[h] = (knope[:, h * HEAD_PAD:(h + 1) * HEAD_PAD] + kr).astype(o_k.dtype)
        o_v[h] = (vfull[:, h * HEAD_PAD:(h + 1) * HEAD_PAD] + ones_col).astype(o_v.dtype)
    if full:
        nq = _rms(_dot(hb, w_q[...]), gq_ref[...]).astype(BF16)
        qf = _dot(nq, w_uq[...])
        for h in range(MLA_HEADS):
            qh = _rope(qf[:, h * HEAD_PAD:(h + 1) * HEAD_PAD], cos, sa, sb) * (MLA_SCALE * LOG2E)
            o_q[h] = qh.astype(o_q.dtype)
        o_hy[...] = _dot(hb, w_hy[...]).astype(o_hy.dtype)
        o_gate[...] = _dot(hb, w_gate[...]).astype(o_gate.dtype)


def _in_proj(x, ge, sh, rope_tabs, wts, full, tm):
    B, L, D = x.shape
    H = MLA_HEADS
    tok = lambda n: pl.BlockSpec((None, tm, n), lambda b, i: (b, i, 0))
    per_b = pl.BlockSpec((None, 1, D), lambda b, i: (b, 0, 0))
    tab = pl.BlockSpec((tm, HEAD_PAD), lambda b, i: (i, 0))
    head = pl.BlockSpec((None, H, tm, HEAD_PAD), lambda b, i: (b, 0, i, 0))
    names = ["w_s5", "w_kv", "w_kr", "g_kv", "w_uk", "w_uv"]
    if full:
        names += ["w_q", "g_q", "w_uq", "w_hy", "w_gate"]
    ws = [wts[n] for n in names]
    in_specs = [tok(D), per_b, per_b, tab, tab, tab] + [_const_spec(w.shape) for w in ws]
    out_specs = [tok(S5_WIDTH), head, head]
    out_shape = [jax.ShapeDtypeStruct((B, L, S5_WIDTH), BF16),
                 jax.ShapeDtypeStruct((B, H, L, HEAD_PAD), BF16),
                 jax.ShapeDtypeStruct((B, H, L, HEAD_PAD), BF16)]
    if full:
        n_hy = (HY_ORDER + 1) * HY_WIDTH
        out_specs += [head, tok(n_hy), tok(3 * D)]
        out_shape += [jax.ShapeDtypeStruct((B, H, L, HEAD_PAD), BF16),
                      jax.ShapeDtypeStruct((B, L, n_hy), BF16),
                      jax.ShapeDtypeStruct((B, L, 3 * D), BF16)]
    return pl.pallas_call(
        functools.partial(_in_kernel, full),
        grid=(B, L // tm),
        in_specs=in_specs, out_specs=out_specs, out_shape=out_shape,
        compiler_params=_cparams(("parallel", "parallel")),
    )(x, ge, sh, *rope_tabs, *ws)


def _attn_kernel(segs, tq, q_ref, *refs):
    o_ref = refs[-1]
    nh = 2
    nsp = 1
    tr = tq // nsp
    chains = [(hh, sp) for hh in range(nh) for sp in range(nsp)]
    qs = [q_ref[hh, sp * tr:(sp + 1) * tr, :] for hh, sp in chains]
    carry = tuple((jnp.full((tr, 1), -jnp.inf, F32), jnp.zeros((tr, HEAD_PAD), F32)) for _ in chains)
    for si, (lk, tk) in enumerate(segs):
        k_ref, v_ref = refs[2 * si], refs[2 * si + 1]

        def body(j, c, k_ref=k_ref, v_ref=v_ref, tk=tk):
            off = pl.multiple_of(j * tk, tk)
            out = []
            for ci, (hh, _) in enumerate(chains):
                m, acc = c[ci]
                kb = k_ref[hh, pl.ds(off, tk), :]
                s = lax.dot_general(qs[ci], kb, (((1,), (1,)), ((), ())), preferred_element_type=F32)
                m_new = jnp.maximum(m, jnp.max(s, axis=-1, keepdims=True))
                p = jnp.exp2(s - m_new).astype(BF16)
                acc = jnp.exp2(m - m_new) * acc + _dot(p, v_ref[hh, pl.ds(off, tk), :])
                out.append((m_new, acc))
            return tuple(out)

        carry = lax.fori_loop(0, lk // tk, body, carry)
    outs = [acc[:, :MLA_V] / acc[:, MLA_V:MLA_V + 1] for _, acc in carry]
    heads = [jnp.concatenate(outs[hh * nsp:(hh + 1) * nsp], axis=0) for hh in range(nh)]
    o_ref[...] = jnp.concatenate(heads, axis=1).astype(o_ref.dtype)


def _attention(q, kvs, tq, tks):
    B, H, Lq, _ = q.shape
    segs = tuple((k.shape[2], tk) for (k, _), tk in zip(kvs, tks))
    in_specs = [pl.BlockSpec((None, 2, tq, HEAD_PAD), lambda b, j, i: (b, j, i, 0))]
    args = [q]
    for k, v in kvs:
        lk = k.shape[2]
        in_specs += [pl.BlockSpec((None, 2, lk, HEAD_PAD), lambda b, j, i: (b, j, 0, 0))] * 2
        args += [k, v]
    return pl.pallas_call(
        functools.partial(_attn_kernel, segs, tq),
        grid=(B, H // 2, Lq // tq),
        in_specs=in_specs,
        out_specs=pl.BlockSpec((None, tq, 2 * MLA_V), lambda b, j, i: (b, i, j)),
        out_shape=jax.ShapeDtypeStruct((B, Lq, H * MLA_V), BF16),
        compiler_params=_cparams(("parallel", "parallel", "arbitrary")),
    )(*args)


def _s5_pack_kernel(cp, p_ref, o_ref):
    x = p_ref[...].astype(F32)
    for kk in range(cp):
        o_ref[kk] = x[kk * S5_CHUNK:(kk + 1) * S5_CHUNK, :].T


def _s5_pack(p):
    B, L, C = p.shape
    T = S5_CHUNK
    nch = L // T
    cp = min(8, nch)
    return pl.pallas_call(
        functools.partial(_s5_pack_kernel, cp),
        grid=(B, nch // cp),
        in_specs=[pl.BlockSpec((None, cp * T, C), lambda b, i: (b, i, 0))],
        out_specs=pl.BlockSpec((None, cp, C, T), lambda b, i: (b, i, 0, 0)),
        out_shape=jax.ShapeDtypeStruct((B, nch, C, T), F32),
        compiler_params=_cparams(("parallel", "parallel")),
    )(p)


def _s5_unpack_kernel(cp, z_ref, o_ref):
    for kk in range(cp):
        o_ref[kk * S5_CHUNK:(kk + 1) * S5_CHUNK, :] = z_ref[kk].T


def _s5_unpack(z):
    B, nch, C, T = z.shape
    cp = min(8, nch)
    return pl.pallas_call(
        functools.partial(_s5_unpack_kernel, cp),
        grid=(B, nch // cp),
        in_specs=[pl.BlockSpec((None, cp, C, T), lambda b, i: (b, i, 0, 0))],
        out_specs=pl.BlockSpec((None, cp * T, C), lambda b, i: (b, i, 0)),
        out_shape=jax.ShapeDtypeStruct((B, nch * T, C), F32),
        compiler_params=_cparams(("parallel", "parallel")),
    )(z)


def _toeplitz_kernel(w_ref, o_ref):
    T, cg = S5_CHUNK, S5_GROUP

    def body(cp, carry):
        r0 = pl.multiple_of(cp * T, T)
        for c in range(cg):
            w = w_ref[pl.ds(cp * cg + c, 1), :]
            blk = pltpu.roll(jnp.broadcast_to(w, (T, 2 * T)), 0, 1, stride=1, stride_axis=0)
            o_ref[pl.ds(r0, T), c * T:(c + 1) * T] = blk[:, :T].astype(BF16)
        return carry

    lax.fori_loop(0, cg, body, 0)


def _toeplitz(w):
    G, npair, W = w.shape
    n = S5_GROUP * S5_CHUNK
    return pl.pallas_call(
        _toeplitz_kernel,
        grid=(G,),
        in_specs=[pl.BlockSpec((None, npair, W), lambda g: (g, 0, 0))],
        out_specs=pl.BlockSpec((None, n, n), lambda g: (g, 0, 0)),
        out_shape=jax.ShapeDtypeStruct((G, n, n), BF16),
        compiler_params=_cparams(("parallel",)),
    )(w)


def _s5_kernel(nch, nb, u_ref, mi_ref, min_ref, mout_ref, atr_ref, ati_ref, h0r_ref, h0i_ref,
               y_ref, hfr_ref, hfi_ref, sr_ref, si_ref, hr_ref, hi_ref):
    ns, cg, T = S5_STATE, S5_GROUP, S5_CHUNK
    u = jnp.concatenate([u_ref[:, c, :] for c in range(cg)], axis=1).astype(BF16)
    s = _dot(u, min_ref[...])
    sr_ref[...] = s[:, :2 * ns]
    si_ref[...] = s[:, 2 * ns:]
    atr, ati = atr_ref[...], ati_ref[...]
    fwd = lax.broadcasted_iota(jnp.int32, (1, 2 * ns), 1) < ns

    def body(i, carry):
        out = []
        for b in range(nb):
            hr, hi = carry[2 * b], carry[2 * b + 1]
            rf = b * nch + i
            rb = b * nch + (nch - 1 - i)
            hr_ref[pl.ds(rf, 1), 0:ns] = hr[:, 0:ns]
            hi_ref[pl.ds(rf, 1), 0:ns] = hi[:, 0:ns]
            hr_ref[pl.ds(rb, 1), ns:2 * ns] = hr[:, ns:2 * ns]
            hi_ref[pl.ds(rb, 1), ns:2 * ns] = hi[:, ns:2 * ns]
            s_r = jnp.where(fwd, sr_ref[pl.ds(rf, 1), :], sr_ref[pl.ds(rb, 1), :])
            s_i = jnp.where(fwd, si_ref[pl.ds(rf, 1), :], si_ref[pl.ds(rb, 1), :])
            out += [atr * hr - ati * hi + s_r, atr * hi + ati * hr + s_i]
        return tuple(out)

    init = []
    for b in range(nb):
        init += [h0r_ref[b:b + 1, :], h0i_ref[b:b + 1, :]]
    fin = lax.fori_loop(0, nch, body, tuple(init))
    for b in range(nb):
        hfr_ref[b:b + 1, :] = fin[2 * b]
        hfi_ref[b:b + 1, :] = fin[2 * b + 1]
    hcat = jnp.concatenate([hr_ref[...], hi_ref[...]], axis=1).astype(BF16)
    cb = 4
    for j in range(cg // cb):
        cols = slice(j * cb * T, (j + 1) * cb * T)
        y = _dot(u, mi_ref[:, cols]) + _dot(hcat, mout_ref[:, cols])
        for c in range(cb):
            y_ref[:, j * cb + c, :] = y[:, c * T:(c + 1) * T]


def _cmul(ar, ai, br, bi):
    return ar * br - ai * bi, ar * bi + ai * br


def _s5_matrices(lam_re, lam_im, log_step, b_re, b_im, c_re, c_im):
    hp = lax.Precision.HIGHEST
    T = S5_CHUNK
    step = jnp.exp(log_step)[..., None]
    def apow(tau):
        mag = jnp.exp(lam_re[:, :, None, :] * step[:, :, None, :] * tau[None, None, :, None])
        ang = lam_im[:, :, None, :] * step[:, :, None, :] * tau[None, None, :, None]
        return mag * jnp.cos(ang), mag * jnp.sin(ang)
    a_r, a_i = apow(jnp.ones((1,), F32))
    a_r, a_i = a_r[:, :, 0], a_i[:, :, 0]
    den = lam_re ** 2 + lam_im ** 2
    q_r, q_i = _cmul(a_r - 1.0, a_i, lam_re / den, -lam_im / den)
    bb_r, bb_i = _cmul(q_r[..., None], q_i[..., None], b_re, b_im)
    tt = jnp.arange(T, dtype=F32)
    p_r, p_i = apow(tt)
    cb_r, cb_i = _cmul(jnp.swapaxes(c_re, -1, -2)[..., :, :, None], jnp.swapaxes(c_im, -1, -2)[..., :, :, None],
                       bb_r[..., :, None, :], bb_i[..., :, None, :])
    kk = (jnp.einsum('dgtn,dgncx->dgtcx', p_r, cb_r, precision=hp)
          - jnp.einsum('dgtn,dgncx->dgtcx', p_i, cb_i, precision=hp))
    kf, kb = kk[0], kk[1]
    G, cg = kf.shape[0], kf.shape[-1]
    wlag = jnp.concatenate([kf[:, :1] + kb[:, :1], kf[:, 1:], jnp.zeros_like(kf[:, :1]), kb[:, :0:-1]], axis=1)
    wlag = jnp.transpose(wlag, (0, 3, 2, 1)).reshape(G, cg * cg, 2 * T)
    m_intra = _toeplitz(wlag)
    def instate(pr, pi, br, bi):
        return _cmul(pr[:, None, :, :], pi[:, None, :, :],
                     jnp.swapaxes(br, -1, -2)[:, :, None, :], jnp.swapaxes(bi, -1, -2)[:, :, None, :])
    f_r, f_i = instate(p_r[0][:, ::-1], p_i[0][:, ::-1], bb_r[0], bb_i[0])
    g_r, g_i = instate(p_r[1], p_i[1], bb_r[1], bb_i[1])
    m_in = jnp.concatenate([f_r, g_r, f_i, g_i], axis=-1).reshape(G, cg * T, 4 * S5_STATE)
    p1_r, p1_i = apow(tt + 1.0)
    def outstate(pr, pi, cr, ci):
        xr, xi = _cmul(jnp.swapaxes(pr, 1, 2)[:, :, None, :], jnp.swapaxes(pi, 1, 2)[:, :, None, :],
                       jnp.swapaxes(cr, 1, 2)[:, :, :, None], jnp.swapaxes(ci, 1, 2)[:, :, :, None])
        return xr, -xi
    of_r, of_i = outstate(p1_r[0], p1_i[0], c_re[0], c_im[0])
    ob_r, ob_i = outstate(p1_r[1][:, ::-1], p1_i[1][:, ::-1], c_re[1], c_im[1])
    m_out = jnp.concatenate([of_r, ob_r, of_i, ob_i], axis=1).reshape(G, 4 * S5_STATE, cg * T)
    at_r, at_i = apow(jnp.full((1,), float(T), F32))
    at_r = jnp.concatenate([at_r[0, :, 0], at_r[1, :, 0]], axis=-1)[:, None, :]
    at_i = jnp.concatenate([at_i[0, :, 0], at_i[1, :, 0]], axis=-1)[:, None, :]
    return m_intra, m_in.astype(BF16), m_out.astype(BF16), at_r, at_i


def _s5_scan(u, mats, h0):
    B, L, C = u.shape
    T, G, cg, ns = S5_CHUNK, S5_GROUPS, S5_GROUP, S5_STATE
    nch = L // T
    R = nch * B
    m_intra, m_in, m_out, at_r, at_i = mats
    z = _s5_pack(u).reshape(R, C, T)
    gspec = lambda r, c: pl.BlockSpec((None, r, c), lambda g: (g, 0, 0))
    zspec = pl.BlockSpec((R, cg, T), lambda g: (0, g, 0))
    y, hfr, hfi = pl.pallas_call(
        functools.partial(_s5_kernel, nch, B),
        grid=(G,),
        in_specs=[zspec, gspec(T * cg, T * cg), gspec(T * cg, 4 * ns), gspec(4 * ns, T * cg),
                  gspec(1, 2 * ns), gspec(1, 2 * ns), gspec(B, 2 * ns), gspec(B, 2 * ns)],
        out_specs=[zspec, gspec(B, 2 * ns), gspec(B, 2 * ns)],
        out_shape=[jax.ShapeDtypeStruct((R, C, T), F32),
                   jax.ShapeDtypeStruct((G, B, 2 * ns), F32),
                   jax.ShapeDtypeStruct((G, B, 2 * ns), F32)],
        scratch_shapes=[pltpu.VMEM((R, 2 * ns), F32)] * 4,
        compiler_params=_cparams(("parallel",)),
    )(z, m_intra, m_in, m_out, at_r, at_i, h0[0], h0[1])
    return _s5_unpack(y.reshape(B, nch, C, T)), (hfr, hfi)


def _shift_rows(x, prev_row, next_row):
    n = x.shape[0]
    row = lax.broadcasted_iota(jnp.int32, x.shape, 0)
    xm = jnp.where(row == 0, prev_row, pltpu.roll(x, 1, 0))
    xp = jnp.where(row == n - 1, next_row, pltpu.roll(x, n - 1, 0))
    return xm, xp


def _dwconv_kernel(p_ref, w_ref, b_ref, o_ref):
    x = p_ref[...].astype(F32)
    w = w_ref[...]
    xm, xp = _shift_rows(x, 0.0, 0.0)
    o_ref[...] = xm * w[0:1] + x * w[1:2] + xp * w[2:3] + b_ref[...]


def _hy_dwconv(p, w, b):
    B, L, n = p.shape
    tc = 128
    nblk = HY_WIDTH // tc
    outs = []
    for part in range(n // HY_WIDTH):
        outs.append(pl.pallas_call(
            _dwconv_kernel,
            grid=(B, nblk),
            in_specs=[pl.BlockSpec((None, L, tc), lambda bb, j, part=part: (bb, 0, part * nblk + j)),
                      pl.BlockSpec((3, tc), lambda bb, j, part=part: (0, part * nblk + j)),
                      pl.BlockSpec((1, tc), lambda bb, j, part=part: (0, part * nblk + j))],
            out_specs=pl.BlockSpec((None, L, tc), lambda bb, j: (bb, 0, j)),
            out_shape=jax.ShapeDtypeStruct((B, L, HY_WIDTH), F32),
            compiler_params=_cparams(("parallel", "parallel")),
        )(p, w, b.reshape(1, n)))
    return outs


def _fft1_kernel(f_ref, u_ref, o_ref):
    o_ref[...] = _dot(f_ref[...], u_ref[...].astype(BF16)).astype(o_ref.dtype)


def _fft1(u2, f1):
    B, K, W = u2.shape
    M = f1.shape[0]
    tn = min(W, 6144)
    return pl.pallas_call(
        _fft1_kernel,
        grid=(B, W // tn),
        in_specs=[_const_spec((M, K)), pl.BlockSpec((None, K, tn), lambda b, j: (b, 0, j))],
        out_specs=pl.BlockSpec((None, M, tn), lambda b, j: (b, 0, j)),
        out_shape=jax.ShapeDtypeStruct((B, M, W), BF16),
        compiler_params=_cparams(("parallel", "parallel")),
    )(f1, u2)


def _fft2_kernel(conv, nbat, f_ref, *rest):
    nb = FFT_NB
    if conv:
        g_ref, h_ref, a_ref, o_ref = rest
        hsp = h_ref[...]
        hr, hi = hsp[:nb], hsp[nb:]
    else:
        a_ref, o_ref = rest
    for b in range(nbat):
        a = jnp.concatenate([a_ref[b, 0], a_ref[b, 1]], axis=0)
        x = _dot(f_ref[...], a)
        if not conv:
            o_ref[b] = x
            continue
        xr, xi = x[:nb], x[nb:]
        y = jnp.concatenate([xr * hr - xi * hi, xr * hi + xi * hr], axis=0)
        z = _dot(g_ref[...], y.astype(BF16)).astype(o_ref.dtype)
        o_ref[b, 0] = z[:nb]
        o_ref[b, 1] = z[nb:]


def _fft2_spectrum(a, fk):
    Bf, _, Na, Nb, C = a.shape
    mat = pl.BlockSpec((None, 2 * Nb, 2 * Nb), lambda k: (k, 0, 0))
    return pl.pallas_call(
        functools.partial(_fft2_kernel, False, Bf),
        grid=(Na,),
        in_specs=[mat, pl.BlockSpec((Bf, 2, None, Nb, C), lambda k: (0, 0, k, 0, 0))],
        out_specs=pl.BlockSpec((Bf, None, 2 * Nb, C), lambda k: (0, k, 0, 0)),
        out_shape=jax.ShapeDtypeStruct((Bf, Na, 2 * Nb, C), F32),
        compiler_params=_cparams(("parallel",)),
    )(fk, a)


def _fft2_conv(a, fk, gk, hspec):
    B, _, Na, Nb, C = a.shape
    mat = pl.BlockSpec((None, 2 * Nb, 2 * Nb), lambda k: (k, 0, 0))
    blk = pl.BlockSpec((B, 2, None, Nb, C), lambda k: (0, 0, k, 0, 0))
    return pl.pallas_call(
        functools.partial(_fft2_kernel, True, B),
        grid=(Na,),
        in_specs=[mat, mat, pl.BlockSpec((None, 2 * Nb, C), lambda k: (k, 0, 0)), blk],
        out_specs=blk,
        out_shape=jax.ShapeDtypeStruct(a.shape, BF16),
        compiler_params=_cparams(("parallel",)),
    )(fk, gk, hspec, a)


def _fft3_kernel(f_ref, b_ref, u_ref, bias_ref, x_ref, o_ref):
    y = _dot(f_ref[...], b_ref[...])
    o_ref[...] = (x_ref[...] * (y + u_ref[...] * bias_ref[...])).astype(o_ref.dtype)


def _fft3(bmat, f3, u2, bias_t, x2):
    B, M2, W = bmat.shape
    K = u2.shape[1]
    tn = min(W, 6144)
    tok = pl.BlockSpec((None, K, tn), lambda b, j: (b, 0, j))
    return pl.pallas_call(
        _fft3_kernel,
        grid=(B, W // tn),
        in_specs=[_const_spec((K, M2)), pl.BlockSpec((None, M2, tn), lambda b, j: (b, 0, j)), tok,
                  pl.BlockSpec((1, tn), lambda b, j: (0, j)), tok],
        out_specs=tok,
        out_shape=jax.ShapeDtypeStruct(u2.shape, F32),
        compiler_params=_cparams(("parallel", "parallel")),
    )(f3, bmat, u2, bias_t, x2)


def _angles(rows, cols, n):
    prod = (rows[:, None] * cols[None, :]) % n
    return prod.astype(F32) * (2.0 * math.pi / n)


def _fft_consts(L):
    N = 2 * L
    Nb = FFT_NB
    Na = N // Nb
    nk = Na // 2 + 1
    nkp = 8 * ((nk + 7) // 8)
    ia = jnp.arange(Na, dtype=jnp.int32)
    ib = jnp.arange(Nb, dtype=jnp.int32)
    ik = jnp.arange(nkp, dtype=jnp.int32)
    valid = (ik < nk).astype(F32)
    th1 = _angles(ik, ia, Na)
    f1 = jnp.concatenate([jnp.cos(th1) * valid[:, None], -jnp.sin(th1) * valid[:, None]], axis=0)
    kfull = ik[:, None] + Na * ib[None, :]
    th2 = ((kfull[:, :, None] * ib[None, None, :]) % N).astype(F32) * (2.0 * math.pi / N)
    c2, s2 = jnp.cos(th2), jnp.sin(th2)
    fk = jnp.concatenate([jnp.concatenate([c2, s2], axis=2),
                          jnp.concatenate([-s2, c2], axis=2)], axis=1)
    c2t, s2t = jnp.swapaxes(c2, 1, 2), jnp.swapaxes(s2, 1, 2)
    gk = jnp.concatenate([jnp.concatenate([c2t, -s2t], axis=2),
                          jnp.concatenate([s2t, c2t], axis=2)], axis=1)
    th3 = _angles(ia[:Na // 2], ik, Na)
    wgt = jnp.where((ik == 0) | (ik == Na // 2), 1.0, 2.0) * valid / N
    f3 = jnp.concatenate([jnp.cos(th3) * wgt[None, :], -jnp.sin(th3) * wgt[None, :]], axis=1)
    bf = lambda a: a.astype(BF16)
    return dict(f1_half=bf(f1[:, :Na // 2]), f1_full=bf(f1), fk=bf(fk), gk=bf(gk), f3=bf(f3),
                Na=Na, Nb=Nb, nkp=nkp)


def _dense_kernel(conv, f_ref, u_ref, *rest):
    u = u_ref[...]
    x = _dot(f_ref[...], u.astype(BF16))
    if not conv:
        rest[0][...] = x
        return
    g_ref, h_ref, bias_ref, x_ref, o_ref = rest
    n = x.shape[0] // 2
    hsp = h_ref[...]
    xr, xi, hr, hi = x[:n], x[n:], hsp[:n], hsp[n:]
    y = jnp.concatenate([xr * hr - xi * hi, xr * hi + xi * hr], axis=0)
    z = _dot(g_ref[...], y.astype(BF16))
    o_ref[...] = (x_ref[...] * (z + u * bias_ref[...])).astype(o_ref.dtype)


def _dense_consts(L):
    N = 2 * L
    i_n = jnp.arange(N, dtype=jnp.int32)
    th = _angles(i_n, i_n, N)
    fd = jnp.concatenate([jnp.cos(th), -jnp.sin(th)], axis=0)
    tht = _angles(i_n[:L], i_n, N)
    fi = jnp.concatenate([jnp.cos(tht), -jnp.sin(tht)], axis=1) / N
    return dict(fd_half=fd[:, :L].astype(BF16), fd_full=fd.astype(BF16), fi=fi.astype(BF16))


def _dense_spectrum(taps, fd):
    Bf, N, C = taps.shape
    return pl.pallas_call(
        functools.partial(_dense_kernel, False),
        grid=(Bf,),
        in_specs=[_const_spec(fd.shape), pl.BlockSpec((None, N, C), lambda b: (b, 0, 0))],
        out_specs=pl.BlockSpec((None, 2 * N, C), lambda b: (b, 0, 0)),
        out_shape=jax.ShapeDtypeStruct((Bf, 2 * N, C), F32),
        compiler_params=_cparams(("parallel",)),
    )(fd, taps)


def _dense_conv(u, fd, fi, hspec, bias, xg):
    B, L, C = u.shape
    tok = pl.BlockSpec((None, L, C), lambda b: (b, 0, 0))
    return pl.pallas_call(
        functools.partial(_dense_kernel, True),
        grid=(B,),
        in_specs=[_const_spec(fd.shape), tok, _const_spec(fi.shape),
                  _const_spec(hspec.shape), _const_spec((1, C)), tok],
        out_specs=tok,
        out_shape=jax.ShapeDtypeStruct((B, L, C), F32),
        compiler_params=_cparams(("parallel",)),
    )(fd, u, fi, hspec, bias.reshape(1, C), xg)


def _hyena_filters(L, f_w1, f_b1, f_w2, f_b2, f_w3, f_freq, f_decay):
    hp = lax.Precision.HIGHEST
    t = jnp.arange(L, dtype=F32)[:, None]
    t_norm = t / L
    bands = jnp.arange(1, HY_BANDS + 1, dtype=F32)
    ang = (2.0 * math.pi / L) * t * bands
    feats = jnp.concatenate([t_norm, jnp.cos(ang), jnp.sin(ang)], axis=-1)
    z = jnp.sin(f_freq * (jnp.dot(feats, f_w1, precision=hp) + f_b1))
    z = jnp.sin(f_freq * (jnp.dot(z, f_w2, precision=hp) + f_b2))
    h = jnp.dot(z, f_w3, precision=hp) * jnp.exp(-t_norm * jnp.abs(f_decay))
    h = h.reshape(L, 2, HY_ORDER, HY_WIDTH)
    h = h / jnp.sum(jnp.abs(h), axis=(0, 1), keepdims=True)
    hf, hb = h[:, 0], h[:, 1]
    taps = jnp.concatenate([hf, jnp.zeros((1, HY_ORDER, HY_WIDTH), F32), hb[:0:-1]], axis=0)
    return jnp.transpose(taps, (1, 0, 2))


def _hyena(p_hy, conv_w, conv_b, filt, bias):
    B, L, _ = p_hy.shape
    C = HY_WIDTH
    taps = _hyena_filters(L, *filt)
    v, x1, x2 = _hy_dwconv(p_hy, conv_w, conv_b)
    gates = (x1, x2)
    u = v
    if 2 * L <= 1024:
        cst = _dense_consts(L)
        spec = _dense_spectrum(taps, cst["fd_full"])
        for o in range(HY_ORDER):
            u = _dense_conv(u, cst["fd_half"], cst["fi"], spec[o], bias[o], gates[o])
        return u
    cst = _fft_consts(L)
    Na, Nb, nkp = cst["Na"], cst["Nb"], cst["nkp"]
    W = Nb * C
    ta = _fft1(taps.reshape(HY_ORDER, Na, W), cst["f1_full"])
    spec = _fft2_spectrum(ta.reshape(HY_ORDER, 2, nkp, Nb, C), cst["fk"])
    for o in range(HY_ORDER):
        u2 = u.reshape(B, Na // 2, W)
        a = _fft1(u2, cst["f1_half"])
        bm = _fft2_conv(a.reshape(B, 2, nkp, Nb, C), cst["fk"], cst["gk"], spec[o])
        bias_t = jnp.tile(bias[o], Nb).reshape(1, W)
        u = _fft3(bm.reshape(B, 2 * nkp, W), cst["f3"], u2, bias_t,
                  gates[o].reshape(B, Na // 2, W)).reshape(B, L, C)
    return u


def _gelu_tanh(x):
    return 0.5 * x * (1.0 + jnp.tanh(math.sqrt(2.0 / math.pi) * (x + 0.044715 * (x * x * x))))


def _merge_kernel(x_ref, g1_ref, pg_ref, yhy_ref, ys5_ref, ps5_ref, d_ref, ymla_ref,
                  wglu_ref, whyr, ws5r, wmlar, wo_ref, o_ref):
    D = x_ref.shape[-1]
    ys = _gelu_tanh(d_ref[...] * ps5_ref[...].astype(F32) + ys5_ref[...])
    ag = _dot(ys.astype(BF16), wglu_ref[...])
    y_s5 = ag[:, :S5_WIDTH] * _sigmoid(ag[:, S5_WIDTH:])
    pg = pg_ref[...].astype(F32)
    merged = (_sigmoid(pg[:, :D]) * _dot(yhy_ref[...].astype(BF16), whyr[...])
              + _sigmoid(pg[:, D:2 * D]) * _dot(y_s5.astype(BF16), ws5r[...])
              + _sigmoid(pg[:, 2 * D:]) * _dot(ymla_ref[...], wmlar[...]))
    o_ref[...] = x_ref[...] + g1_ref[...] * _dot(merged.astype(BF16), wo_ref[...])


def _merge(x, g1, pg, y_hy, ys5, ps5, d_skip, y_mla, wts, tm):
    B, L, D = x.shape
    tok = lambda n: pl.BlockSpec((None, tm, n), lambda b, i: (b, i, 0))
    per_b = pl.BlockSpec((None, 1, D), lambda b, i: (b, 0, 0))
    ws = [wts[n] for n in ("w_glu", "w_br_hy", "w_br_s5", "w_br_mla", "w_o")]
    return pl.pallas_call(
        _merge_kernel,
        grid=(B, L // tm),
        in_specs=[tok(D), per_b, tok(3 * D), tok(HY_WIDTH), tok(S5_WIDTH), tok(S5_WIDTH),
                  _const_spec((1, S5_WIDTH)), tok(MLA_HEADS * MLA_V)] + [_const_spec(w.shape) for w in ws],
        out_specs=tok(D),
        out_shape=jax.ShapeDtypeStruct((B, L, D), F32),
        compiler_params=_cparams(("parallel", "parallel")),
    )(x, g1, pg, y_hy, ys5, ps5, d_skip.reshape(1, -1), y_mla, *ws)


def _ffn_up_kernel(nt, x_ref, xp_ref, xn_ref, ge_ref, sh_ref, w_ref, cw_ref, cb_ref, o_ref):
    i = pl.program_id(1)
    tm = x_ref.shape[0]
    hid = w_ref.shape[1] // 2
    x = jnp.concatenate([xp_ref[...], x_ref[...], xn_ref[...]], axis=0)
    hb = (_rms(x, ge_ref[...]) + sh_ref[...]).astype(BF16)
    a = _dot(hb, w_ref[...])
    row8 = lax.broadcasted_iota(jnp.int32, (8, 1), 0)
    top = jnp.where(jnp.logical_and(row8 == 7, i == 0), 0.0, a[0:8])
    bot = jnp.where(jnp.logical_and(row8 == 0, i == nt - 1), 0.0, a[tm + 8:tm + 16])
    a = jnp.concatenate([top, a[8:tm + 8], bot], axis=0)
    am = pltpu.roll(a, 1, 0)[8:tm + 8]
    ap = pltpu.roll(a, tm + 15, 0)[8:tm + 8]
    cw = cw_ref[...]
    c = am * cw[0:1] + a[8:tm + 8] * cw[1:2] + ap * cw[2:3] + cb_ref[...]
    o_ref[...] = (_silu(c[:, :hid]) * c[:, hid:]).astype(o_ref.dtype)


def _ffn_up(x, ge, sh, w_up, conv_w, conv_b, tm):
    B, L, D = x.shape
    n = w_up.shape[1]
    nt = L // tm
    r8 = tm // 8
    per_b = pl.BlockSpec((None, 1, D), lambda b, i: (b, 0, 0))
    return pl.pallas_call(
        functools.partial(_ffn_up_kernel, nt),
        grid=(B, nt),
        in_specs=[pl.BlockSpec((None, tm, D), lambda b, i: (b, i, 0)),
                  pl.BlockSpec((None, 8, D), lambda b, i: (b, jnp.maximum(i * r8 - 1, 0), 0)),
                  pl.BlockSpec((None, 8, D), lambda b, i: (b, jnp.minimum((i + 1) * r8, L // 8 - 1), 0)),
                  per_b, per_b, _const_spec(w_up.shape), _const_spec((3, n)), _const_spec((1, n))],
        out_specs=pl.BlockSpec((None, tm, n // 2), lambda b, i: (b, i, 0)),
        out_shape=jax.ShapeDtypeStruct((B, L, n // 2), BF16),
        compiler_params=_cparams(("parallel", "parallel")),
    )(x, x, x, ge, sh, w_up, conv_w, conv_b.reshape(1, n))


def _ffn_down_kernel(final, act_ref, wd_ref, x_ref, g2_ref, *rest):
    y = x_ref[...] + g2_ref[...] * _dot(act_ref[...], wd_ref[...])
    if final:
        fg_ref, o_ref = rest
        o_ref[...] = _rms(y, fg_ref[...])
    else:
        rest[0][...] = y


def _ffn_down(act, w_down, x, g2, final_g, tm):
    B, L, n = act.shape
    D = x.shape[-1]
    final = final_g is not None
    in_specs = [pl.BlockSpec((None, tm, n), lambda b, i: (b, i, 0)), _const_spec(w_down.shape),
                pl.BlockSpec((None, tm, D), lambda b, i: (b, i, 0)),
                pl.BlockSpec((None, 1, D), lambda b, i: (b, 0, 0))]
    args = [act, w_down, x, g2]
    if final:
        in_specs.append(_const_spec((1, D)))
        args.append(final_g.reshape(1, D))
    return pl.pallas_call(
        functools.partial(_ffn_down_kernel, final),
        grid=(B, L // tm),
        in_specs=in_specs,
        out_specs=pl.BlockSpec((None, tm, D), lambda b, i: (b, i, 0)),
        out_shape=jax.ShapeDtypeStruct((B, L, D), F32),
        compiler_params=_cparams(("parallel", "parallel")),
    )(*args)


def _rope_tables(L, rope):
    half = MLA_ROPE // 2
    cos = jnp.ones((L, HEAD_PAD), F32)
    sa = jnp.zeros((L, HEAD_PAD), F32)
    sb = jnp.zeros((L, HEAD_PAD), F32)
    if not rope:
        return cos, sa, sb
    rows = L // GRID_W
    row = jnp.repeat(jnp.arange(rows, dtype=F32), GRID_W)
    col = jnp.tile(jnp.arange(GRID_W, dtype=F32), rows)
    n_ax = MLA_ROPE // 4
    inv = ROPE_THETA ** (-jnp.arange(n_ax, dtype=F32) / n_ax)
    ang = jnp.concatenate([row[:, None] * inv, col[:, None] * inv], axis=-1)
    c, s = jnp.cos(ang), jnp.sin(ang)
    cos = cos.at[:, MLA_NOPE:MLA_NOPE + half].set(c).at[:, MLA_NOPE + half:MLA_NOPE + 2 * half].set(c)
    sa = sa.at[:, MLA_NOPE:MLA_NOPE + half].set(-s)
    sb = sb.at[:, MLA_NOPE + half:MLA_NOPE + 2 * half].set(s)
    return cos, sa, sb


def _layer_weights(w_in, g_q, w_uq, g_kv, w_ukv, s5_w_glu, w_br_hy, w_br_s5, w_br_mla, w_o):
    D = w_in.shape[0]
    H = MLA_HEADS
    c_kv = S5_WIDTH
    c_kr = c_kv + w_ukv.shape[0]
    c_q = c_kr + MLA_ROPE
    c_hy = c_q + w_uq.shape[0]
    c_gate = c_hy + (HY_ORDER + 1) * HY_WIDTH
    bf = lambda a: a.astype(BF16)
    w_kr = jnp.zeros((D, HEAD_PAD), F32).at[:, MLA_NOPE:MLA_NOPE + MLA_ROPE].set(w_in[:, c_kr:c_q])
    ukv = w_ukv.reshape(-1, H, MLA_NOPE + MLA_V)
    w_uk = jnp.zeros((ukv.shape[0], H, HEAD_PAD), F32).at[:, :, :MLA_NOPE].set(ukv[:, :, :MLA_NOPE])
    w_uv = jnp.zeros((ukv.shape[0], H, HEAD_PAD), F32).at[:, :, :MLA_V].set(ukv[:, :, MLA_NOPE:])
    uq = w_uq.reshape(-1, H, MLA_NOPE + MLA_ROPE)
    w_uqp = jnp.zeros((uq.shape[0], H, HEAD_PAD), F32).at[:, :, :MLA_NOPE + MLA_ROPE].set(uq)
    return dict(
        w_s5=bf(w_in[:, :c_kv]), w_kv=bf(w_in[:, c_kv:c_kr]), w_kr=bf(w_kr), w_q=bf(w_in[:, c_q:c_hy]),
        w_hy=bf(w_in[:, c_hy:c_gate]), w_gate=bf(w_in[:, c_gate:]),
        g_kv=g_kv.reshape(1, -1), g_q=g_q.reshape(1, -1),
        w_uk=bf(w_uk.reshape(-1, H * HEAD_PAD)), w_uv=bf(w_uv.reshape(-1, H * HEAD_PAD)),
        w_uq=bf(w_uqp.reshape(-1, H * HEAD_PAD)),
        w_glu=bf(s5_w_glu), w_br_hy=bf(w_br_hy), w_br_s5=bf(w_br_s5), w_br_mla=bf(w_br_mla), w_o=bf(w_o))


def _mixer(x, mods, n1g, wts, s5_mats, s5_h0, s5_d, hy_args, rope_tabs, ctx_kv, full, tm, tq, tk):
    sh1, sc1, g1 = mods
    ge = n1g * (1.0 + sc1)
    outs = _in_proj(x, ge, sh1, rope_tabs, wts, full, tm)
    p_s5, k, v = outs[:3]
    y_s5, hfin = _s5_scan(p_s5, s5_mats, s5_h0)
    if not full:
        return None, hfin, (k, v)
    q, p_hy, p_gate = outs[3:]
    kvs = [(k, v)] + ([ctx_kv] if ctx_kv is not None else [])
    tks = [tk] + ([ctx_kv[0].shape[2]] if ctx_kv is not None else [])
    y_mla = _attention(q, kvs, tq, tks)
    y_hy = _hyena(p_hy, *hy_args)
    x_new = _merge(x, g1, p_gate, y_hy, y_s5, p_s5, s5_d, y_mla, wts, tm)
    return x_new, hfin, (k, v)


def _ffn(x, mods, n2g, w_up, conv_w, conv_b, w_down, final_g, tm):
    sh2, sc2, g2 = mods
    act = _ffn_up(x, n2g * (1.0 + sc2), sh2, w_up, conv_w, conv_b, tm)
    return _ffn_down(act, w_down, x, g2, final_g, min(2 * tm, x.shape[1]))


def kernel(x, c, ctx, c_ctx, w_mod, b_mod, norm1_g, norm2_g, w_in, hy_conv_w, hy_conv_b, hy_f_w1, hy_f_b1, hy_f_w2, hy_f_b2, hy_f_w3, hy_f_freq, hy_f_decay, hy_bias, s5_lam_re, s5_lam_im, s5_log_step, s5_b_re, s5_b_im, s5_c_re, s5_c_im, s5_d, s5_w_glu, mla_g_q, mla_w_uq, mla_g_kv, mla_w_ukv, w_br_hy, w_br_s5, w_br_mla, w_o, ffn_w_up, ffn_conv_w, ffn_conv_b, ffn_w_down, final_g):
    B, L, D = x.shape
    Lc = ctx.shape[1]
    depth = w_mod.shape[0]
    tm = min(512, L)
    tmf = min(256, L)
    tmc = min(256, Lc)
    tq = min(1024, L)
    tk = min(2048, L)
    rope_lat = _rope_tables(L, True)
    rope_ctx = _rope_tables(Lc, False)
    mrows = 8 * ((B + 1 + 7) // 8)
    c_all = jnp.zeros((mrows, D), F32).at[:B].set(c).at[B].set(c_ctx)
    zeros_h = (jnp.zeros((S5_GROUPS, B, 2 * S5_STATE), F32),) * 2
    xc = ctx
    for i in range(depth):
        ctx_out = i < depth - 1
        mod = _modulation(c_all, w_mod[i], b_mod[i])
        m_lat = [m[:, None, :] for m in jnp.split(mod[:B], 6, axis=-1)]
        m_ctx = [jnp.broadcast_to(m[None, None, :], (B, 1, D)) for m in jnp.split(mod[B], 6, axis=-1)]
        wts = _layer_weights(w_in[i], mla_g_q[i], mla_w_uq[i], mla_g_kv[i], mla_w_ukv[i], s5_w_glu[i],
                             w_br_hy[i], w_br_s5[i], w_br_mla[i], w_o[i])
        s5_mats = _s5_matrices(s5_lam_re[i], s5_lam_im[i], s5_log_step[i], s5_b_re[i], s5_b_im[i],
                               s5_c_re[i], s5_c_im[i])
        filt = (hy_f_w1[i], hy_f_b1[i], hy_f_w2[i], hy_f_b2[i], hy_f_w3[i], hy_f_freq[i], hy_f_decay[i])
        hy_args = (hy_conv_w[i], hy_conv_b[i], filt, hy_bias[i])
        n1g, n2g = norm1_g[i][None, None, :], norm2_g[i][None, None, :]
        w_up, w_down = ffn_w_up[i].astype(BF16), ffn_w_down[i].astype(BF16)

        xc_new, hc_fin, ctx_kv = _mixer(xc, m_ctx[0:3], n1g, wts, s5_mats, zeros_h, s5_d[i], hy_args,
                                        rope_ctx, None, ctx_out, tmc, tmc, tmc)
        x, _, _ = _mixer(x, m_lat[0:3], n1g, wts, s5_mats, hc_fin, s5_d[i], hy_args,
                         rope_lat, ctx_kv, True, tm, tq, tk)
        last = i == depth - 1
        x = _ffn(x, m_lat[3:6], n2g, w_up, ffn_conv_w[i], ffn_conv_b[i], w_down,
                 final_g if last else None, tmf)
        if ctx_out:
            xc = _ffn(xc_new, m_ctx[3:6], n2g, w_up, ffn_conv_w[i], ffn_conv_b[i], w_down, None, tmc)
    return x
```

```python
import functools
import math

import jax
import jax.numpy as jnp
from jax import lax
from jax.experimental import pallas as pl
from jax.experimental.pallas import tpu as pltpu

F32 = jnp.float32
BF16 = jnp.bfloat16

EPS = 1e-6
GRID_W = 64
HY_WIDTH = 384
HY_ORDER = 2
HY_BANDS = 8
S5_WIDTH = 384
S5_GROUP = 16
S5_GROUPS = S5_WIDTH // S5_GROUP
S5_STATE = 64
S5_CHUNK = 128
MLA_HEADS = 8
MLA_NOPE = 64
MLA_ROPE = 32
MLA_V = 64
MLA_SCALE = (MLA_NOPE + MLA_ROPE) ** -0.5
LOG2E = math.log2(math.e)
ROPE_THETA = 10000.0
HEAD_PAD = 128
FFT_NB = 128
VMEM_LIMIT = 56 * 1024 * 1024


def _cparams(sem):
    return pltpu.CompilerParams(dimension_semantics=sem, vmem_limit_bytes=VMEM_LIMIT)


def _split(x):
    hi = x.astype(BF16)
    lo = (x - hi.astype(F32)).astype(BF16)
    return hi, lo


def _dot(a, b):
    return jnp.dot(a, b, preferred_element_type=F32)


def _dot3(ah, al, bh, bl):
    return _dot(ah, bh) + _dot(al, bh) + _dot(ah, bl)


def _sigmoid(x):
    return 1.0 / (1.0 + jnp.exp(-x))


def _silu(x):
    return x * _sigmoid(x)


def _rms(x, g):
    return x * lax.rsqrt(jnp.mean(x * x, axis=-1, keepdims=True) + EPS) * g


def _const_spec(shape):
    nd = len(shape)
    return pl.BlockSpec(shape, lambda *_: (0,) * nd, pipeline_mode=pl.Buffered(1))


def _mod_kernel(c_ref, w_ref, b_ref, o_ref):
    ch, cl = _split(_silu(c_ref[...]))
    wh, wl = _split(w_ref[...])
    o_ref[...] = _dot3(ch, cl, wh, wl) + b_ref[...]


def _modulation(c_all, w, b):
    m, d = c_all.shape
    n = w.shape[1]
    tn = 1024
    return pl.pallas_call(
        _mod_kernel,
        grid=(n // tn,),
        in_specs=[pl.BlockSpec((m, d), lambda j: (0, 0)),
                  pl.BlockSpec((d, tn), lambda j: (0, j)),
                  pl.BlockSpec((1, tn), lambda j: (0, j))],
        out_specs=pl.BlockSpec((m, tn), lambda j: (0, j)),
        out_shape=jax.ShapeDtypeStruct((m, n), F32),
        compiler_params=_cparams(("arbitrary",)),
    )(c_all, w, b.reshape(1, n))


def _rope(x, cos, sa, sb):
    return (x * cos + pltpu.roll(x, HEAD_PAD - MLA_ROPE // 2, 1) * sa
            + pltpu.roll(x, MLA_ROPE // 2, 1) * sb)


def _in_kernel(full, x_ref, ge_ref, sh_ref, cos_ref, sa_ref, sb_ref,
               w_s5, w_kv, w_kr, gkv_ref, w_uk, w_uv, *rest):
    if full:
        (w_q, gq_ref, w_uq, w_hy, w_gate,
         o_s5, o_k, o_v, o_q, o_hy, o_gate) = rest
    else:
        o_s5, o_k, o_v = rest
    hb = (_rms(x_ref[...], ge_ref[...]) + sh_ref[...]).astype(BF16)
    cos, sa, sb = cos_ref[...], sa_ref[...], sb_ref[...]
    o_s5[...] = _dot(hb, w_s5[...]).astype(o_s5.dtype)
    nkv = _rms(_dot(hb, w_kv[...]), gkv_ref[...]).astype(BF16)
    vfull = _dot(nkv, w_uv[...])
    knope = _dot(nkv, w_uk[...])
    kr = _rope(_dot(hb, w_kr[...]), cos, sa, sb)
    ones_col = (lax.broadcasted_iota(jnp.int32, (1, HEAD_PAD), 1) == MLA_V).astype(F32)
    for h in range(MLA_HEADS):
        o_k[h] = (knope[:, h * HEAD_PAD:(h + 1) * HEAD_PAD] + kr).astype(o_k.dtype)
        o_v[h] = (vfull[:, h * HEAD_PAD:(h + 1) * HEAD_PAD] + ones_col).astype(o_v.dtype)
    if full:
        nq = _rms(_dot(hb, w_q[...]), gq_ref[...]).astype(BF16)
        qf = _dot(nq, w_uq[...])
        for h in range(MLA_HEADS):
            qh = _rope(qf[:, h * HEAD_PAD:(h + 1) * HEAD_PAD], cos, sa, sb) * (MLA_SCALE * LOG2E)
            o_q[h] = qh.astype(o_q.dtype)
        o_hy[...] = _dot(hb, w_hy[...]).astype(o_hy.dtype)
        o_gate[...] = _dot(hb, w_gate[...]).astype(o_gate.dtype)


def _in_proj(x, ge, sh, rope_tabs, wts, full, tm):
    B, L, D = x.shape
    H = MLA_HEADS
    tok = lambda n: pl.BlockSpec((None, tm, n), lambda b, i: (b, i, 0))
    per_b = pl.BlockSpec((None, 1, D), lambda b, i: (b, 0, 0))
    tab = pl.BlockSpec((tm, HEAD_PAD), lambda b, i: (i, 0))
    head = pl.BlockSpec((None, H, tm, HEAD_PAD), lambda b, i: (b, 0, i, 0))
    names = ["w_s5", "w_kv", "w_kr", "g_kv", "w_uk", "w_uv"]
    if full:
        names += ["w_q", "g_q", "w_uq", "w_hy", "w_gate"]
    ws = [wts[n] for n in names]
    in_specs = [tok(D), per_b, per_b, tab, tab, tab] + [_const_spec(w.shape) for w in ws]
    out_specs = [tok(S5_WIDTH), head, head]
    out_shape = [jax.ShapeDtypeStruct((B, L, S5_WIDTH), BF16),
                 jax.ShapeDtypeStruct((B, H, L, HEAD_PAD), BF16),
                 jax.ShapeDtypeStruct((B, H, L, HEAD_PAD), BF16)]
    if full:
        n_hy = (HY_ORDER + 1) * HY_WIDTH
        out_specs += [head, tok(n_hy), tok(3 * D)]
        out_shape += [jax.ShapeDtypeStruct((B, H, L, HEAD_PAD), BF16),
                      jax.ShapeDtypeStruct((B, L, n_hy), BF16),
                      jax.ShapeDtypeStruct((B, L, 3 * D), BF16)]
    return pl.pallas_call(
        functools.partial(_in_kernel, full),
        grid=(B, L // tm),
        in_specs=in_specs, out_specs=out_specs, out_shape=out_shape,
        compiler_params=_cparams(("parallel", "parallel")),
    )(x, ge, sh, *rope_tabs, *ws)


def _attn_kernel(segs, tq, q_ref, *refs):
    o_ref = refs[-1]
    nh = 2
    nsp = 1
    tr = tq // nsp
    chains = [(hh, sp) for hh in range(nh) for sp in range(nsp)]
    qs = [q_ref[hh, sp * tr:(sp + 1) * tr, :] for hh, sp in chains]
    carry = tuple((jnp.full((tr, 1), -jnp.inf, F32), jnp.zeros((tr, HEAD_PAD), F32)) for _ in chains)
    for si, (lk, tk) in enumerate(segs):
        k_ref, v_ref = refs[2 * si], refs[2 * si + 1]

        def body(j, c, k_ref=k_ref, v_ref=v_ref, tk=tk):
            off = pl.multiple_of(j * tk, tk)
            out = []
            for ci, (hh, _) in enumerate(chains):
                m, acc = c[ci]
                kb = k_ref[hh, pl.ds(off, tk), :]
                s = lax.dot_general(qs[ci], kb, (((1,), (1,)), ((), ())), preferred_element_type=F32)
                m_new = jnp.maximum(m, jnp.max(s, axis=-1, keepdims=True))
                p = jnp.exp2(s - m_new).astype(BF16)
                acc = jnp.exp2(m - m_new) * acc + _dot(p, v_ref[hh, pl.ds(off, tk), :])
                out.append((m_new, acc))
            return tuple(out)

        carry = lax.fori_loop(0, lk // tk, body, carry)
    outs = [acc[:, :MLA_V] / acc[:, MLA_V:MLA_V + 1] for _, acc in carry]
    heads = [jnp.concatenate(outs[hh * nsp:(hh + 1) * nsp], axis=0) for hh in range(nh)]
    o_ref[...] = jnp.concatenate(heads, axis=1).astype(o_ref.dtype)


def _attention(q, kvs, tq, tks):
    B, H, Lq, _ = q.shape
    segs = tuple((k.shape[2], tk) for (k, _), tk in zip(kvs, tks))
    in_specs = [pl.BlockSpec((None, 2, tq, HEAD_PAD), lambda b, j, i: (b, j, i, 0))]
    args = [q]
    for k, v in kvs:
        lk = k.shape[2]
        in_specs += [pl.BlockSpec((None, 2, lk, HEAD_PAD), lambda b, j, i: (b, j, 0, 0))] * 2
        args += [k, v]
    return pl.pallas_call(
        functools.partial(_attn_kernel, segs, tq),
        grid=(B, H // 2, Lq // tq),
        in_specs=in_specs,
        out_specs=pl.BlockSpec((None, tq, 2 * MLA_V), lambda b, j, i: (b, i, j)),
        out_shape=jax.ShapeDtypeStruct((B, Lq, H * MLA_V), BF16),
        compiler_params=_cparams(("parallel", "parallel", "arbitrary")),
    )(*args)


def _s5_pack_kernel(cp, p_ref, o_ref):
    x = p_ref[...].astype(F32)
    for kk in range(cp):
        o_ref[kk] = x[kk * S5_CHUNK:(kk + 1) * S5_CHUNK, :].T


def _s5_pack(p):
    B, L, C = p.shape
    T = S5_CHUNK
    nch = L // T
    cp = min(8, nch)
    return pl.pallas_call(
        functools.partial(_s5_pack_kernel, cp),
        grid=(B, nch // cp),
        in_specs=[pl.BlockSpec((None, cp * T, C), lambda b, i: (b, i, 0))],
        out_specs=pl.BlockSpec((None, cp, C, T), lambda b, i: (b, i, 0, 0)),
        out_shape=jax.ShapeDtypeStruct((B, nch, C, T), F32),
        compiler_params=_cparams(("parallel", "parallel")),
    )(p)


def _s5_unpack_kernel(cp, z_ref, o_ref):
    for kk in range(cp):
        o_ref[kk * S5_CHUNK:(kk + 1) * S5_CHUNK, :] = z_ref[kk].T


def _s5_unpack(z):
    B, nch, C, T = z.shape
    cp = min(8, nch)
    return pl.pallas_call(
        functools.partial(_s5_unpack_kernel, cp),
        grid=(B, nch // cp),
        in_specs=[pl.BlockSpec((None, cp, C, T), lambda b, i: (b, i, 0, 0))],
        out_specs=pl.BlockSpec((None, cp * T, C), lambda b, i: (b, i, 0)),
        out_shape=jax.ShapeDtypeStruct((B, nch * T, C), F32),
        compiler_params=_cparams(("parallel", "parallel")),
    )(z)


def _toeplitz_kernel(w_ref, o_ref):
    T, cg = S5_CHUNK, S5_GROUP

    def body(cp, carry):
        r0 = pl.multiple_of(cp * T, T)
        for c in range(cg):
            w = w_ref[pl.ds(cp * cg + c, 1), :]
            blk = pltpu.roll(jnp.broadcast_to(w, (T, 2 * T)), 0, 1, stride=1, stride_axis=0)
            o_ref[pl.ds(r0, T), c * T:(c + 1) * T] = blk[:, :T].astype(BF16)
        return carry

    lax.fori_loop(0, cg, body, 0)


def _toeplitz(w):
    G, npair, W = w.shape
    n = S5_GROUP * S5_CHUNK
    return pl.pallas_call(
        _toeplitz_kernel,
        grid=(G,),
        in_specs=[pl.BlockSpec((None, npair, W), lambda g: (g, 0, 0))],
        out_specs=pl.BlockSpec((None, n, n), lambda g: (g, 0, 0)),
        out_shape=jax.ShapeDtypeStruct((G, n, n), BF16),
        compiler_params=_cparams(("parallel",)),
    )(w)


def _s5_kernel(nch, nb, u_ref, mi_ref, min_ref, mout_ref, atr_ref, ati_ref, h0r_ref, h0i_ref,
               y_ref, hfr_ref, hfi_ref, sr_ref, si_ref, hr_ref, hi_ref):
    ns, cg, T = S5_STATE, S5_GROUP, S5_CHUNK
    u = jnp.concatenate([u_ref[:, c, :] for c in range(cg)], axis=1).astype(BF16)
    s = _dot(u, min_ref[...])
    sr_ref[...] = s[:, :2 * ns]
    si_ref[...] = s[:, 2 * ns:]
    atr, ati = atr_ref[...], ati_ref[...]
    fwd = lax.broadcasted_iota(jnp.int32, (1, 2 * ns), 1) < ns

    def body(i, carry):
        out = []
        for b in range(nb):
            hr, hi = carry[2 * b], carry[2 * b + 1]
            rf = b * nch + i
            rb = b * nch + (nch - 1 - i)
            hr_ref[pl.ds(rf, 1), 0:ns] = hr[:, 0:ns]
            hi_ref[pl.ds(rf, 1), 0:ns] = hi[:, 0:ns]
            hr_ref[pl.ds(rb, 1), ns:2 * ns] = hr[:, ns:2 * ns]
            hi_ref[pl.ds(rb, 1), ns:2 * ns] = hi[:, ns:2 * ns]
            s_r = jnp.where(fwd, sr_ref[pl.ds(rf, 1), :], sr_ref[pl.ds(rb, 1), :])
            s_i = jnp.where(fwd, si_ref[pl.ds(rf, 1), :], si_ref[pl.ds(rb, 1), :])
            out += [atr * hr - ati * hi + s_r, atr * hi + ati * hr + s_i]
        return tuple(out)

    init = []
    for b in range(nb):
        init += [h0r_ref[b:b + 1, :], h0i_ref[b:b + 1, :]]
    fin = lax.fori_loop(0, nch, body, tuple(init))
    for b in range(nb):
        hfr_ref[b:b + 1, :] = fin[2 * b]
        hfi_ref[b:b + 1, :] = fin[2 * b + 1]
    hcat = jnp.concatenate([hr_ref[...], hi_ref[...]], axis=1).astype(BF16)
    cb = 4
    for j in range(cg // cb):
        cols = slice(j * cb * T, (j + 1) * cb * T)
        y = _dot(u, mi_ref[:, cols]) + _dot(hcat, mout_ref[:, cols])
        for c in range(cb):
            y_ref[:, j * cb + c, :] = y[:, c * T:(c + 1) * T]


def _cmul(ar, ai, br, bi):
    return ar * br - ai * bi, ar * bi + ai * br


def _s5_matrices(lam_re, lam_im, log_step, b_re, b_im, c_re, c_im):
    hp = lax.Precision.HIGHEST
    T = S5_CHUNK
    step = jnp.exp(log_step)[..., None]
    def apow(tau):
        mag = jnp.exp(lam_re[:, :, None, :] * step[:, :, None, :] * tau[None, None, :, None])
        ang = lam_im[:, :, None, :] * step[:, :, None, :] * tau[None, None, :, None]
        return mag * jnp.cos(ang), mag * jnp.sin(ang)
    a_r, a_i = apow(jnp.ones((1,), F32))
    a_r, a_i = a_r[:, :, 0], a_i[:, :, 0]
    den = lam_re ** 2 + lam_im ** 2
    q_r, q_i = _cmul(a_r - 1.0, a_i, lam_re / den, -lam_im / den)
    bb_r, bb_i = _cmul(q_r[..., None], q_i[..., None], b_re, b_im)
    tt = jnp.arange(T, dtype=F32)
    p_r, p_i = apow(tt)
    cb_r, cb_i = _cmul(jnp.swapaxes(c_re, -1, -2)[..., :, :, None], jnp.swapaxes(c_im, -1, -2)[..., :, :, None],
                       bb_r[..., :, None, :], bb_i[..., :, None, :])
    kk = (jnp.einsum('dgtn,dgncx->dgtcx', p_r, cb_r, precision=hp)
          - jnp.einsum('dgtn,dgncx->dgtcx', p_i, cb_i, precision=hp))
    kf, kb = kk[0], kk[1]
    G, cg = kf.shape[0], kf.shape[-1]
    wlag = jnp.concatenate([kf[:, :1] + kb[:, :1], kf[:, 1:], jnp.zeros_like(kf[:, :1]), kb[:, :0:-1]], axis=1)
    wlag = jnp.transpose(wlag, (0, 3, 2, 1)).reshape(G, cg * cg, 2 * T)
    m_intra = _toeplitz(wlag)
    def instate(pr, pi, br, bi):
        return _cmul(pr[:, None, :, :], pi[:, None, :, :],
                     jnp.swapaxes(br, -1, -2)[:, :, None, :], jnp.swapaxes(bi, -1, -2)[:, :, None, :])
    f_r, f_i = instate(p_r[0][:, ::-1], p_i[0][:, ::-1], bb_r[0], bb_i[0])
    g_r, g_i = instate(p_r[1], p_i[1], bb_r[1], bb_i[1])
    m_in = jnp.concatenate([f_r, g_r, f_i, g_i], axis=-1).reshape(G, cg * T, 4 * S5_STATE)
    p1_r, p1_i = apow(tt + 1.0)
    def outstate(pr, pi, cr, ci):
        xr, xi = _cmul(jnp.swapaxes(pr, 1, 2)[:, :, None, :], jnp.swapaxes(pi, 1, 2)[:, :, None, :],
                       jnp.swapaxes(cr, 1, 2)[:, :, :, None], jnp.swapaxes(ci, 1, 2)[:, :, :, None])
        return xr, -xi
    of_r, of_i = outstate(p1_r[0], p1_i[0], c_re[0], c_im[0])
    ob_r, ob_i = outstate(p1_r[1][:, ::-1], p1_i[1][:, ::-1], c_re[1], c_im[1])
    m_out = jnp.concatenate([of_r, ob_r, of_i, ob_i], axis=1).reshape(G, 4 * S5_STATE, cg * T)
    at_r, at_i = apow(jnp.full((1,), float(T), F32))
    at_r = jnp.concatenate([at_r[0, :, 0], at_r[1, :, 0]], axis=-1)[:, None, :]
    at_i = jnp.concatenate([at_i[0, :, 0], at_i[1, :, 0]], axis=-1)[:, None, :]
    return m_intra, m_in.astype(BF16), m_out.astype(BF16), at_r, at_i


def _s5_scan(u, mats, h0):
    B, L, C = u.shape
    T, G, cg, ns = S5_CHUNK, S5_GROUPS, S5_GROUP, S5_STATE
    nch = L // T
    R = nch * B
    m_intra, m_in, m_out, at_r, at_i = mats
    z = _s5_pack(u).reshape(R, C, T)
    gspec = lambda r, c: pl.BlockSpec((None, r, c), lambda g: (g, 0, 0))
    zspec = pl.BlockSpec((R, cg, T), lambda g: (0, g, 0))
    y, hfr, hfi = pl.pallas_call(
        functools.partial(_s5_kernel, nch, B),
        grid=(G,),
        in_specs=[zspec, gspec(T * cg, T * cg), gspec(T * cg, 4 * ns), gspec(4 * ns, T * cg),
                  gspec(1, 2 * ns), gspec(1, 2 * ns), gspec(B, 2 * ns), gspec(B, 2 * ns)],
        out_specs=[zspec, gspec(B, 2 * ns), gspec(B, 2 * ns)],
        out_shape=[jax.ShapeDtypeStruct((R, C, T), F32),
                   jax.ShapeDtypeStruct((G, B, 2 * ns), F32),
                   jax.ShapeDtypeStruct((G, B, 2 * ns), F32)],
        scratch_shapes=[pltpu.VMEM((R, 2 * ns), F32)] * 4,
        compiler_params=_cparams(("parallel",)),
    )(z, m_intra, m_in, m_out, at_r, at_i, h0[0], h0[1])
    return _s5_unpack(y.reshape(B, nch, C, T)), (hfr, hfi)


def _shift_rows(x, prev_row, next_row):
    n = x.shape[0]
    row = lax.broadcasted_iota(jnp.int32, x.shape, 0)
    xm = jnp.where(row == 0, prev_row, pltpu.roll(x, 1, 0))
    xp = jnp.where(row == n - 1, next_row, pltpu.roll(x, n - 1, 0))
    return xm, xp


def _dwconv_kernel(p_ref, w_ref, b_ref, o_ref):
    x = p_ref[...].astype(F32)
    w = w_ref[...]
    xm, xp = _shift_rows(x, 0.0, 0.0)
    o_ref[...] = xm * w[0:1] + x * w[1:2] + xp * w[2:3] + b_ref[...]


def _hy_dwconv(p, w, b):
    B, L, n = p.shape
    tc = 128
    nblk = HY_WIDTH // tc
    outs = []
    for part in range(n // HY_WIDTH):
        outs.append(pl.pallas_call(
            _dwconv_kernel,
            grid=(B, nblk),
            in_specs=[pl.BlockSpec((None, L, tc), lambda bb, j, part=part: (bb, 0, part * nblk + j)),
                      pl.BlockSpec((3, tc), lambda bb, j, part=part: (0, part * nblk + j)),
                      pl.BlockSpec((1, tc), lambda bb, j, part=part: (0, part * nblk + j))],
            out_specs=pl.BlockSpec((None, L, tc), lambda bb, j: (bb, 0, j)),
            out_shape=jax.ShapeDtypeStruct((B, L, HY_WIDTH), F32),
            compiler_params=_cparams(("parallel", "parallel")),
        )(p, w, b.reshape(1, n)))
    return outs


def _fft1_kernel(f_ref, u_ref, o_ref):
    o_ref[...] = _dot(f_ref[...], u_ref[...].astype(BF16)).astype(o_ref.dtype)


def _fft1(u2, f1):
    B, K, W = u2.shape
    M = f1.shape[0]
    tn = min(W, 6144)
    return pl.pallas_call(
        _fft1_kernel,
        grid=(B, W // tn),
        in_specs=[_const_spec((M, K)), pl.BlockSpec((None, K, tn), lambda b, j: (b, 0, j))],
        out_specs=pl.BlockSpec((None, M, tn), lambda b, j: (b, 0, j)),
        out_shape=jax.ShapeDtypeStruct((B, M, W), BF16),
        compiler_params=_cparams(("parallel", "parallel")),
    )(f1, u2)


def _fft2_kernel(conv, nbat, f_ref, *rest):
    nb = FFT_NB
    if conv:
        g_ref, h_ref, a_ref, o_ref = rest
        hsp = h_ref[...]
        hr, hi = hsp[:nb], hsp[nb:]
    else:
        a_ref, o_ref = rest
    for b in range(nbat):
        a = jnp.concatenate([a_ref[b, 0], a_ref[b, 1]], axis=0)
        x = _dot(f_ref[...], a)
        if not conv:
            o_ref[b] = x
            continue
        xr, xi = x[:nb], x[nb:]
        y = jnp.concatenate([xr * hr - xi * hi, xr * hi + xi * hr], axis=0)
        z = _dot(g_ref[...], y.astype(BF16)).astype(o_ref.dtype)
        o_ref[b, 0] = z[:nb]
        o_ref[b, 1] = z[nb:]


def _fft2_spectrum(a, fk):
    Bf, _, Na, Nb, C = a.shape
    mat = pl.BlockSpec((None, 2 * Nb, 2 * Nb), lambda k: (k, 0, 0))
    return pl.pallas_call(
        functools.partial(_fft2_kernel, False, Bf),
        grid=(Na,),
        in_specs=[mat, pl.BlockSpec((Bf, 2, None, Nb, C), lambda k: (0, 0, k, 0, 0))],
        out_specs=pl.BlockSpec((Bf, None, 2 * Nb, C), lambda k: (0, k, 0, 0)),
        out_shape=jax.ShapeDtypeStruct((Bf, Na, 2 * Nb, C), F32),
        compiler_params=_cparams(("parallel",)),
    )(fk, a)


def _fft2_conv(a, fk, gk, hspec):
    B, _, Na, Nb, C = a.shape
    mat = pl.BlockSpec((None, 2 * Nb, 2 * Nb), lambda k: (k, 0, 0))
    blk = pl.BlockSpec((B, 2, None, Nb, C), lambda k: (0, 0, k, 0, 0))
    return pl.pallas_call(
        functools.partial(_fft2_kernel, True, B),
        grid=(Na,),
        in_specs=[mat, mat, pl.BlockSpec((None, 2 * Nb, C), lambda k: (k, 0, 0)), blk],
        out_specs=blk,
        out_shape=jax.ShapeDtypeStruct(a.shape, BF16),
        compiler_params=_cparams(("parallel",)),
    )(fk, gk, hspec, a)


def _fft3_kernel(f_ref, b_ref, u_ref, bias_ref, x_ref, o_ref):
    y = _dot(f_ref[...], b_ref[...])
    o_ref[...] = (x_ref[...] * (y + u_ref[...] * bias_ref[...])).astype(o_ref.dtype)


def _fft3(bmat, f3, u2, bias_t, x2):
    B, M2, W = bmat.shape
    K = u2.shape[1]
    tn = min(W, 6144)
    tok = pl.BlockSpec((None, K, tn), lambda b, j: (b, 0, j))
    return pl.pallas_call(
        _fft3_kernel,
        grid=(B, W // tn),
        in_specs=[_const_spec((K, M2)), pl.BlockSpec((None, M2, tn), lambda b, j: (b, 0, j)), tok,
                  pl.BlockSpec((1, tn), lambda b, j: (0, j)), tok],
        out_specs=tok,
        out_shape=jax.ShapeDtypeStruct(u2.shape, F32),
        compiler_params=_cparams(("parallel", "parallel")),
    )(f3, bmat, u2, bias_t, x2)


def _angles(rows, cols, n):
    prod = (rows[:, None] * cols[None, :]) % n
    return prod.astype(F32) * (2.0 * math.pi / n)


def _fft_consts(L):
    N = 2 * L
    Nb = FFT_NB
    Na = N // Nb
    nk = Na // 2 + 1
    nkp = 8 * ((nk + 7) // 8)
    ia = jnp.arange(Na, dtype=jnp.int32)
    ib = jnp.arange(Nb, dtype=jnp.int32)
    ik = jnp.arange(nkp, dtype=jnp.int32)
    valid = (ik < nk).astype(F32)
    th1 = _angles(ik, ia, Na)
    f1 = jnp.concatenate([jnp.cos(th1) * valid[:, None], -jnp.sin(th1) * valid[:, None]], axis=0)
    kfull = ik[:, None] + Na * ib[None, :]
    th2 = ((kfull[:, :, None] * ib[None, None, :]) % N).astype(F32) * (2.0 * math.pi / N)
    c2, s2 = jnp.cos(th2), jnp.sin(th2)
    fk = jnp.concatenate([jnp.concatenate([c2, s2], axis=2),
                          jnp.concatenate([-s2, c2], axis=2)], axis=1)
    c2t, s2t = jnp.swapaxes(c2, 1, 2), jnp.swapaxes(s2, 1, 2)
    gk = jnp.concatenate([jnp.concatenate([c2t, -s2t], axis=2),
                          jnp.concatenate([s2t, c2t], axis=2)], axis=1)
    th3 = _angles(ia[:Na // 2], ik, Na)
    wgt = jnp.where((ik == 0) | (ik == Na // 2), 1.0, 2.0) * valid / N
    f3 = jnp.concatenate([jnp.cos(th3) * wgt[None, :], -jnp.sin(th3) * wgt[None, :]], axis=1)
    bf = lambda a: a.astype(BF16)
    return dict(f1_half=bf(f1[:, :Na // 2]), f1_full=bf(f1), fk=bf(fk), gk=bf(gk), f3=bf(f3),
                Na=Na, Nb=Nb, nkp=nkp)


def _dense_kernel(conv, f_ref, u_ref, *rest):
    u = u_ref[...]
    x = _dot(f_ref[...], u.astype(BF16))
    if not conv:
        rest[0][...] = x
        return
    g_ref, h_ref, bias_ref, x_ref, o_ref = rest
    n = x.shape[0] // 2
    hsp = h_ref[...]
    xr, xi, hr, hi = x[:n], x[n:], hsp[:n], hsp[n:]
    y = jnp.concatenate([xr * hr - xi * hi, xr * hi + xi * hr], axis=0)
    z = _dot(g_ref[...], y.astype(BF16))
    o_ref[...] = (x_ref[...] * (z + u * bias_ref[...])).astype(o_ref.dtype)


def _dense_consts(L):
    N = 2 * L
    i_n = jnp.arange(N, dtype=jnp.int32)
    th = _angles(i_n, i_n, N)
    fd = jnp.concatenate([jnp.cos(th), -jnp.sin(th)], axis=0)
    tht = _angles(i_n[:L], i_n, N)
    fi = jnp.concatenate([jnp.cos(tht), -jnp.sin(tht)], axis=1) / N
    return dict(fd_half=fd[:, :L].astype(BF16), fd_full=fd.astype(BF16), fi=fi.astype(BF16))


def _dense_spectrum(taps, fd):
    Bf, N, C = taps.shape
    return pl.pallas_call(
        functools.partial(_dense_kernel, False),
        grid=(Bf,),
        in_specs=[_const_spec(fd.shape), pl.BlockSpec((None, N, C), lambda b: (b, 0, 0))],
        out_specs=pl.BlockSpec((None, 2 * N, C), lambda b: (b, 0, 0)),
        out_shape=jax.ShapeDtypeStruct((Bf, 2 * N, C), F32),
        compiler_params=_cparams(("parallel",)),
    )(fd, taps)


def _dense_conv(u, fd, fi, hspec, bias, xg):
    B, L, C = u.shape
    tok = pl.BlockSpec((None, L, C), lambda b: (b, 0, 0))
    return pl.pallas_call(
        functools.partial(_dense_kernel, True),
        grid=(B,),
        in_specs=[_const_spec(fd.shape), tok, _const_spec(fi.shape),
                  _const_spec(hspec.shape), _const_spec((1, C)), tok],
        out_specs=tok,
        out_shape=jax.ShapeDtypeStruct((B, L, C), F32),
        compiler_params=_cparams(("parallel",)),
    )(fd, u, fi, hspec, bias.reshape(1, C), xg)


def _hyena_filters(L, f_w1, f_b1, f_w2, f_b2, f_w3, f_freq, f_decay):
    hp = lax.Precision.HIGHEST
    bands = jnp.arange(1, HY_BANDS + 1, dtype=F32)

    def taps_at(t, w3, decay):
        t_norm = t / L
        ang = (2.0 * math.pi / L) * t * bands
        feats = jnp.concatenate([t_norm, jnp.cos(ang), jnp.sin(ang)], axis=-1)
        z = jnp.sin(f_freq * (jnp.dot(feats, f_w1, precision=hp) + f_b1))
        z = jnp.sin(f_freq * (jnp.dot(z, f_w2, precision=hp) + f_b2))
        return jnp.dot(z, w3, precision=hp) * jnp.exp(-t_norm * jnp.abs(decay))

    ncol = HY_ORDER * HY_WIDTH
    h = taps_at(jnp.arange(L, dtype=F32)[:, None], f_w3, f_decay).reshape(L, 2, HY_ORDER, HY_WIDTH)
    norm = jnp.sum(jnp.abs(h), axis=(0, 1))
    hf = h[:, 0] / norm
    t_rev = jnp.arange(L - 1, 0, -1, dtype=F32)[:, None]
    hb_rev = taps_at(t_rev, f_w3[:, ncol:], f_decay[ncol:]).reshape(L - 1, HY_ORDER, HY_WIDTH) / norm
    taps = jnp.concatenate([hf, jnp.zeros((1, HY_ORDER, HY_WIDTH), F32), hb_rev], axis=0)
    return jnp.transpose(taps, (1, 0, 2))


def _hyena(p_hy, conv_w, conv_b, filt, bias):
    B, L, _ = p_hy.shape
    C = HY_WIDTH
    taps = _hyena_filters(L, *filt)
    v, x1, x2 = _hy_dwconv(p_hy, conv_w, conv_b)
    gates = (x1, x2)
    u = v
    if 2 * L <= 1024:
        cst = _dense_consts(L)
        spec = _dense_spectrum(taps, cst["fd_full"])
        for o in range(HY_ORDER):
            u = _dense_conv(u, cst["fd_half"], cst["fi"], spec[o], bias[o], gates[o])
        return u
    cst = _fft_consts(L)
    Na, Nb, nkp = cst["Na"], cst["Nb"], cst["nkp"]
    W = Nb * C
    ta = _fft1(taps.reshape(HY_ORDER, Na, W), cst["f1_full"])
    spec = _fft2_spectrum(ta.reshape(HY_ORDER, 2, nkp, Nb, C), cst["fk"])
    for o in range(HY_ORDER):
        u2 = u.reshape(B, Na // 2, W)
        a = _fft1(u2, cst["f1_half"])
        bm = _fft2_conv(a.reshape(B, 2, nkp, Nb, C), cst["fk"], cst["gk"], spec[o])
        bias_t = jnp.tile(bias[o], Nb).reshape(1, W)
        u = _fft3(bm.reshape(B, 2 * nkp, W), cst["f3"], u2, bias_t,
                  gates[o].reshape(B, Na // 2, W)).reshape(B, L, C)
    return u


def _gelu_tanh(x):
    return 0.5 * x * (1.0 + jnp.tanh(math.sqrt(2.0 / math.pi) * (x + 0.044715 * (x * x * x))))


def _merge_kernel(x_ref, g1_ref, pg_ref, yhy_ref, ys5_ref, ps5_ref, d_ref, ymla_ref,
                  wglu_ref, whyr, ws5r, wmlar, wo_ref, o_ref):
    D = x_ref.shape[-1]
    ys = _gelu_tanh(d_ref[...] * ps5_ref[...].astype(F32) + ys5_ref[...])
    ag = _dot(ys.astype(BF16), wglu_ref[...])
    y_s5 = ag[:, :S5_WIDTH] * _sigmoid(ag[:, S5_WIDTH:])
    pg = pg_ref[...].astype(F32)
    merged = (_sigmoid(pg[:, :D]) * _dot(yhy_ref[...].astype(BF16), whyr[...])
              + _sigmoid(pg[:, D:2 * D]) * _dot(y_s5.astype(BF16), ws5r[...])
              + _sigmoid(pg[:, 2 * D:]) * _dot(ymla_ref[...], wmlar[...]))
    o_ref[...] = x_ref[...] + g1_ref[...] * _dot(merged.astype(BF16), wo_ref[...])


def _merge(x, g1, pg, y_hy, ys5, ps5, d_skip, y_mla, wts, tm):
    B, L, D = x.shape
    tok = lambda n: pl.BlockSpec((None, tm, n), lambda b, i: (b, i, 0))
    per_b = pl.BlockSpec((None, 1, D), lambda b, i: (b, 0, 0))
    ws = [wts[n] for n in ("w_glu", "w_br_hy", "w_br_s5", "w_br_mla", "w_o")]
    return pl.pallas_call(
        _merge_kernel,
        grid=(B, L // tm),
        in_specs=[tok(D), per_b, tok(3 * D), tok(HY_WIDTH), tok(S5_WIDTH), tok(S5_WIDTH),
                  _const_spec((1, S5_WIDTH)), tok(MLA_HEADS * MLA_V)] + [_const_spec(w.shape) for w in ws],
        out_specs=tok(D),
        out_shape=jax.ShapeDtypeStruct((B, L, D), F32),
        compiler_params=_cparams(("parallel", "parallel")),
    )(x, g1, pg, y_hy, ys5, ps5, d_skip.reshape(1, -1), y_mla, *ws)


def _ffn_up_kernel(nt, x_ref, xp_ref, xn_ref, ge_ref, sh_ref, w_ref, cw_ref, cb_ref, o_ref):
    i = pl.program_id(1)
    tm = x_ref.shape[0]
    hid = w_ref.shape[1] // 2
    x = jnp.concatenate([xp_ref[...], x_ref[...], xn_ref[...]], axis=0)
    hb = (_rms(x, ge_ref[...]) + sh_ref[...]).astype(BF16)
    a = _dot(hb, w_ref[...])
    row8 = lax.broadcasted_iota(jnp.int32, (8, 1), 0)
    top = jnp.where(jnp.logical_and(row8 == 7, i == 0), 0.0, a[0:8])
    bot = jnp.where(jnp.logical_and(row8 == 0, i == nt - 1), 0.0, a[tm + 8:tm + 16])
    a = jnp.concatenate([top, a[8:tm + 8], bot], axis=0)
    am = pltpu.roll(a, 1, 0)[8:tm + 8]
    ap = pltpu.roll(a, tm + 15, 0)[8:tm + 8]
    cw = cw_ref[...]
    c = am * cw[0:1] + a[8:tm + 8] * cw[1:2] + ap * cw[2:3] + cb_ref[...]
    o_ref[...] = (_silu(c[:, :hid]) * c[:, hid:]).astype(o_ref.dtype)


def _ffn_up(x, ge, sh, w_up, conv_w, conv_b, tm):
    B, L, D = x.shape
    n = w_up.shape[1]
    nt = L // tm
    r8 = tm // 8
    per_b = pl.BlockSpec((None, 1, D), lambda b, i: (b, 0, 0))
    return pl.pallas_call(
        functools.partial(_ffn_up_kernel, nt),
        grid=(B, nt),
        in_specs=[pl.BlockSpec((None, tm, D), lambda b, i: (b, i, 0)),
                  pl.BlockSpec((None, 8, D), lambda b, i: (b, jnp.maximum(i * r8 - 1, 0), 0)),
                  pl.BlockSpec((None, 8, D), lambda b, i: (b, jnp.minimum((i + 1) * r8, L // 8 - 1), 0)),
                  per_b, per_b, _const_spec(w_up.shape), _const_spec((3, n)), _const_spec((1, n))],
        out_specs=pl.BlockSpec((None, tm, n // 2), lambda b, i: (b, i, 0)),
        out_shape=jax.ShapeDtypeStruct((B, L, n // 2), BF16),
        compiler_params=_cparams(("parallel", "parallel")),
    )(x, x, x, ge, sh, w_up, conv_w, conv_b.reshape(1, n))


def _ffn_down_kernel(final, act_ref, wd_ref, x_ref, g2_ref, *rest):
    y = x_ref[...] + g2_ref[...] * _dot(act_ref[...], wd_ref[...])
    if final:
        fg_ref, o_ref = rest
        o_ref[...] = _rms(y, fg_ref[...])
    else:
        rest[0][...] = y


def _ffn_down(act, w_down, x, g2, final_g, tm):
    B, L, n = act.shape
    D = x.shape[-1]
    final = final_g is not None
    in_specs = [pl.BlockSpec((None, tm, n), lambda b, i: (b, i, 0)), _const_spec(w_down.shape),
                pl.BlockSpec((None, tm, D), lambda b, i: (b, i, 0)),
                pl.BlockSpec((None, 1, D), lambda b, i: (b, 0, 0))]
    args = [act, w_down, x, g2]
    if final:
        in_specs.append(_const_spec((1, D)))
        args.append(final_g.reshape(1, D))
    return pl.pallas_call(
        functools.partial(_ffn_down_kernel, final),
        grid=(B, L // tm),
        in_specs=in_specs,
        out_specs=pl.BlockSpec((None, tm, D), lambda b, i: (b, i, 0)),
        out_shape=jax.ShapeDtypeStruct((B, L, D), F32),
        compiler_params=_cparams(("parallel", "parallel")),
    )(*args)


def _rope_tables(L, rope):
    half = MLA_ROPE // 2
    cos = jnp.ones((L, HEAD_PAD), F32)
    sa = jnp.zeros((L, HEAD_PAD), F32)
    sb = jnp.zeros((L, HEAD_PAD), F32)
    if not rope:
        return cos, sa, sb
    rows = L // GRID_W
    row = jnp.repeat(jnp.arange(rows, dtype=F32), GRID_W)
    col = jnp.tile(jnp.arange(GRID_W, dtype=F32), rows)
    n_ax = MLA_ROPE // 4
    inv = ROPE_THETA ** (-jnp.arange(n_ax, dtype=F32) / n_ax)
    ang = jnp.concatenate([row[:, None] * inv, col[:, None] * inv], axis=-1)
    c, s = jnp.cos(ang), jnp.sin(ang)
    cos = cos.at[:, MLA_NOPE:MLA_NOPE + half].set(c).at[:, MLA_NOPE + half:MLA_NOPE + 2 * half].set(c)
    sa = sa.at[:, MLA_NOPE:MLA_NOPE + half].set(-s)
    sb = sb.at[:, MLA_NOPE + half:MLA_NOPE + 2 * half].set(s)
    return cos, sa, sb


def _layer_weights(w_in, g_q, w_uq, g_kv, w_ukv, s5_w_glu, w_br_hy, w_br_s5, w_br_mla, w_o):
    D = w_in.shape[0]
    H = MLA_HEADS
    c_kv = S5_WIDTH
    c_kr = c_kv + w_ukv.shape[0]
    c_q = c_kr + MLA_ROPE
    c_hy = c_q + w_uq.shape[0]
    c_gate = c_hy + (HY_ORDER + 1) * HY_WIDTH
    bf = lambda a: a.astype(BF16)
    w_kr = jnp.zeros((D, HEAD_PAD), F32).at[:, MLA_NOPE:MLA_NOPE + MLA_ROPE].set(w_in[:, c_kr:c_q])
    ukv = w_ukv.reshape(-1, H, MLA_NOPE + MLA_V)
    w_uk = jnp.zeros((ukv.shape[0], H, HEAD_PAD), F32).at[:, :, :MLA_NOPE].set(ukv[:, :, :MLA_NOPE])
    w_uv = jnp.zeros((ukv.shape[0], H, HEAD_PAD), F32).at[:, :, :MLA_V].set(ukv[:, :, MLA_NOPE:])
    uq = w_uq.reshape(-1, H, MLA_NOPE + MLA_ROPE)
    w_uqp = jnp.zeros((uq.shape[0], H, HEAD_PAD), F32).at[:, :, :MLA_NOPE + MLA_ROPE].set(uq)
    return dict(
        w_s5=bf(w_in[:, :c_kv]), w_kv=bf(w_in[:, c_kv:c_kr]), w_kr=bf(w_kr), w_q=bf(w_in[:, c_q:c_hy]),
        w_hy=bf(w_in[:, c_hy:c_gate]), w_gate=bf(w_in[:, c_gate:]),
        g_kv=g_kv.reshape(1, -1), g_q=g_q.reshape(1, -1),
        w_uk=bf(w_uk.reshape(-1, H * HEAD_PAD)), w_uv=bf(w_uv.reshape(-1, H * HEAD_PAD)),
        w_uq=bf(w_uqp.reshape(-1, H * HEAD_PAD)),
        w_glu=bf(s5_w_glu), w_br_hy=bf(w_br_hy), w_br_s5=bf(w_br_s5), w_br_mla=bf(w_br_mla), w_o=bf(w_o))


def _mixer(x, mods, n1g, wts, s5_mats, s5_h0, s5_d, hy_args, rope_tabs, ctx_kv, full, tm, tq, tk):
    sh1, sc1, g1 = mods
    ge = n1g * (1.0 + sc1)
    outs = _in_proj(x, ge, sh1, rope_tabs, wts, full, tm)
    p_s5, k, v = outs[:3]
    y_s5, hfin = _s5_scan(p_s5, s5_mats, s5_h0)
    if not full:
        return None, hfin, (k, v)
    q, p_hy, p_gate = outs[3:]
    kvs = [(k, v)] + ([ctx_kv] if ctx_kv is not None else [])
    tks = [tk] + ([ctx_kv[0].shape[2]] if ctx_kv is not None else [])
    y_mla = _attention(q, kvs, tq, tks)
    y_hy = _hyena(p_hy, *hy_args)
    x_new = _merge(x, g1, p_gate, y_hy, y_s5, p_s5, s5_d, y_mla, wts, tm)
    return x_new, hfin, (k, v)


def _ffn(x, mods, n2g, w_up, conv_w, conv_b, w_down, final_g, tm):
    sh2, sc2, g2 = mods
    act = _ffn_up(x, n2g * (1.0 + sc2), sh2, w_up, conv_w, conv_b, tm)
    return _ffn_down(act, w_down, x, g2, final_g, min(2 * tm, x.shape[1]))


def kernel(x, c, ctx, c_ctx, w_mod, b_mod, norm1_g, norm2_g, w_in, hy_conv_w, hy_conv_b, hy_f_w1, hy_f_b1, hy_f_w2, hy_f_b2, hy_f_w3, hy_f_freq, hy_f_decay, hy_bias, s5_lam_re, s5_lam_im, s5_log_step, s5_b_re, s5_b_im, s5_c_re, s5_c_im, s5_d, s5_w_glu, mla_g_q, mla_w_uq, mla_g_kv, mla_w_ukv, w_br_hy, w_br_s5, w_br_mla, w_o, ffn_w_up, ffn_conv_w, ffn_conv_b, ffn_w_down, final_g):
    B, L, D = x.shape
    Lc = ctx.shape[1]
    depth = w_mod.shape[0]
    tm = min(512, L)
    tmf = min(256, L)
    tmc = min(256, Lc)
    tq = min(1024, L)
    tk = min(2048, L)
    rope_lat = _rope_tables(L, True)
    rope_ctx = _rope_tables(Lc, False)
    mrows = 8 * ((B + 1 + 7) // 8)
    c_all = jnp.zeros((mrows, D), F32).at[:B].set(c).at[B].set(c_ctx)
    zeros_h = (jnp.zeros((S5_GROUPS, B, 2 * S5_STATE), F32),) * 2
    xc = ctx
    for i in range(depth):
        ctx_out = i < depth - 1
        mod = _modulation(c_all, w_mod[i], b_mod[i])
        m_lat = [m[:, None, :] for m in jnp.split(mod[:B], 6, axis=-1)]
        m_ctx = [jnp.broadcast_to(m[None, None, :], (B, 1, D)) for m in jnp.split(mod[B], 6, axis=-1)]
        wts = _layer_weights(w_in[i], mla_g_q[i], mla_w_uq[i], mla_g_kv[i], mla_w_ukv[i], s5_w_glu[i],
                             w_br_hy[i], w_br_s5[i], w_br_mla[i], w_o[i])
        s5_mats = _s5_matrices(s5_lam_re[i], s5_lam_im[i], s5_log_step[i], s5_b_re[i], s5_b_im[i],
                               s5_c_re[i], s5_c_im[i])
        filt = (hy_f_w1[i], hy_f_b1[i], hy_f_w2[i], hy_f_b2[i], hy_f_w3[i], hy_f_freq[i], hy_f_decay[i])
        hy_args = (hy_conv_w[i], hy_conv_b[i], filt, hy_bias[i])
        n1g, n2g = norm1_g[i][None, None, :], norm2_g[i][None, None, :]
        w_up, w_down = ffn_w_up[i].astype(BF16), ffn_w_down[i].astype(BF16)

        xc_new, hc_fin, ctx_kv = _mixer(xc, m_ctx[0:3], n1g, wts, s5_mats, zeros_h, s5_d[i], hy_args,
                                        rope_ctx, None, ctx_out, tmc, tmc, tmc)
        x, _, _ = _mixer(x, m_lat[0:3], n1g, wts, s5_mats, hc_fin, s5_d[i], hy_args,
                         rope_lat, ctx_kv, True, tm, tq, tk)
        last = i == depth - 1
        x = _ffn(x, m_lat[3:6], n2g, w_up, ffn_conv_w[i], ffn_conv_b[i], w_down,
                 final_g if last else None, tmf)
        if ctx_out:
            xc = _ffn(xc_new, m_ctx[3:6], n2g, w_up, ffn_conv_w[i], ffn_conv_b[i], w_down, None, tmc)
    return x
```
